```python
import math
import jax, jax.numpy as jnp
from jax import lax
import numpy as np

D_MODEL = 4096
BATCH = 2
SEQ = 8192
DEPTH = 4

ATTN_HEADS = 32
ATTN_KV_HEADS = 4
ATTN_HEAD_DIM = 64
ATTN_GROUP = ATTN_HEADS // ATTN_KV_HEADS
WINDOW = 128
BLOCK = 128
ATTN_SCALE = ATTN_HEAD_DIM ** -0.5
NUM_BUCKETS = 32
MAX_DISTANCE = 128
HG_HEADS = 8
HG_EXPAND = 128
HG_HEAD_V = 128
HG_CHUNK = 64
D_FF = 4 * D_MODEL
EPS = 1e-6

ATTN_Q_DIM = ATTN_HEADS * ATTN_HEAD_DIM
ATTN_KV_DIM = ATTN_KV_HEADS * ATTN_HEAD_DIM
HG_K_DIM = HG_HEADS * HG_EXPAND
HG_V_DIM = HG_HEADS * HG_HEAD_V
MIX_DIM = ATTN_Q_DIM + HG_V_DIM
SPLIT_SIZES = (ATTN_Q_DIM, ATTN_KV_DIM, ATTN_KV_DIM, HG_K_DIM, HG_K_DIM, HG_V_DIM, HG_V_DIM, D_MODEL, D_MODEL)
SPLIT_POINTS = tuple(sum(SPLIT_SIZES[:i + 1]) for i in range(len(SPLIT_SIZES) - 1))
IN_DIM = sum(SPLIT_SIZES)

kernel_name = "hybrid_swa_hgrn2_gated_merge"


def rms_norm(x, gain):
    xf = x.astype(jnp.float32)
    y = xf * lax.rsqrt(jnp.mean(xf * xf, axis=-1, keepdims=True) + EPS)
    return (y * gain.astype(jnp.float32)).astype(x.dtype)


def t5_bucket(dist):
    max_exact = NUM_BUCKETS // 2
    d = jnp.maximum(dist, 1).astype(jnp.float32)
    large = max_exact + (jnp.log(d / max_exact) / math.log(MAX_DISTANCE / max_exact)
                         * (NUM_BUCKETS - max_exact)).astype(jnp.int32)
    return jnp.where(dist < max_exact, dist, jnp.minimum(large, NUM_BUCKETS - 1))


def relative_bias_band(rel_bias, n_blocks):
    qi = jnp.arange(BLOCK)[:, None]
    ki = jnp.arange(2 * BLOCK)[None, :]
    dist = qi + BLOCK - ki
    in_window = (dist >= 0) & (dist < WINDOW)
    bucket = t5_bucket(jnp.maximum(dist, 0))
    bias = jnp.transpose(rel_bias[bucket], (2, 0, 1)).astype(jnp.float32)
    blk = jnp.arange(n_blocks)[:, None, None]
    mask = in_window[None] & ((blk > 0) | (ki[None] >= BLOCK))
    return bias, mask


def sliding_window_attention(q, k, v, sinks, bias, mask):
    B, S = q.shape[:2]
    nb = S // BLOCK
    qb = q.reshape(B, nb, BLOCK, ATTN_KV_HEADS, ATTN_GROUP, ATTN_HEAD_DIM)

    def band(t):
        tp = jnp.pad(t, ((0, 0), (BLOCK, 0), (0, 0), (0, 0))).reshape(B, nb + 1, BLOCK, ATTN_KV_HEADS, ATTN_HEAD_DIM)
        return jnp.concatenate([tp[:, :-1], tp[:, 1:]], axis=2)

    kb, vb = band(k), band(v)
    s = jnp.einsum('bnqkgd,bnskd->bnkgqs', qb, kb, preferred_element_type=jnp.float32) * ATTN_SCALE
    s = s + bias.reshape(ATTN_KV_HEADS, ATTN_GROUP, BLOCK, 2 * BLOCK)
    s = jnp.where(mask[None, :, None, None], s, -jnp.inf)
    sink = sinks.astype(jnp.float32).reshape(1, 1, ATTN_KV_HEADS, ATTN_GROUP, 1, 1)
    m = jnp.maximum(jnp.max(s, axis=-1, keepdims=True), sink)
    p = jnp.exp(s - m)
    p = p / (jnp.sum(p, axis=-1, keepdims=True) + jnp.exp(sink - m))
    o = jnp.einsum('bnkgqs,bnskd->bnqkgd', p.astype(v.dtype), vb)
    return o.reshape(B, S, ATTN_Q_DIM)


def hgrn2_chunk_scan(q, k, logf, v):
    B, S, H, DK = q.shape
    DV = v.shape[-1]
    nc = S // HG_CHUNK

    def to_chunks(t):
        return t.reshape(B, nc, HG_CHUNK, H, t.shape[-1]).transpose(1, 0, 3, 2, 4)

    causal = jnp.tril(jnp.ones((HG_CHUNK, HG_CHUNK), dtype=bool))

    def step(state, chunk):
        qc, kc, lc, vc = chunk
        b = jnp.cumsum(lc, axis=2)
        o_inter = jnp.einsum('bhtd,bhde->bhte', qc * jnp.exp(b), state)
        rel = jnp.where(causal[:, :, None], b[:, :, :, None, :] - b[:, :, None, :, :], -jnp.inf)
        scores = jnp.einsum('bhtd,bhtsd,bhsd->bhts', qc, jnp.exp(rel), kc)
        o_intra = jnp.einsum('bhts,bhse->bhte', scores, vc)
        b_last = b[:, :, -1, :]
        k_to_end = kc * jnp.exp(b_last[:, :, None, :] - b)
        new_state = jnp.exp(b_last)[..., None] * state + jnp.einsum('bhsd,bhse->bhde', k_to_end, vc)
        return new_state, o_inter + o_intra

    state0 = jnp.zeros((B, H, DK, DV), jnp.float32)
    _, o = lax.scan(step, state0, (to_chunks(q), to_chunks(k), to_chunks(logf), to_chunks(v)))
    return o.transpose(1, 0, 3, 2, 4).reshape(B, S, H, DV)


def hgrn2(q, f_logit, i, g, lower_bound, norm_gain):
    B, S, _ = q.shape
    qf = jax.nn.silu(q.astype(jnp.float32)).reshape(B, S, HG_HEADS, HG_EXPAND)
    forget = lower_bound + (1.0 - lower_bound) * jax.nn.sigmoid(f_logit.astype(jnp.float32))
    logf = jnp.log(forget).reshape(B, S, HG_HEADS, HG_EXPAND)
    k = (1.0 - forget).reshape(B, S, HG_HEADS, HG_EXPAND)
    v = i.astype(jnp.float32).reshape(B, S, HG_HEADS, HG_HEAD_V)
    o = hgrn2_chunk_scan(qf, k, logf, v)
    o = rms_norm(o, norm_gain) * jax.nn.silu(g.astype(jnp.float32).reshape(B, S, HG_HEADS, HG_HEAD_V))
    return o.reshape(B, S, HG_V_DIM).astype(q.dtype)


def setup_inputs(seed: int = 0) -> dict:
    key = jax.random.key(seed)
    ks = jax.random.split(key, 16)

    def normal(k, shape, scale):
        return jax.random.normal(k, shape, jnp.float32) * scale

    row_scale = jnp.concatenate([jnp.full((ATTN_Q_DIM, 1), ATTN_Q_DIM ** -0.5, jnp.float32),
                                 jnp.full((HG_V_DIM, 1), HG_V_DIM ** -0.5, jnp.float32)], axis=0)
    return {
        "x": normal(ks[0], (BATCH, SEQ, D_MODEL), 1.0),
        "attn_norm_gain": 1.0 + normal(ks[1], (DEPTH, D_MODEL), 0.1),
        "w_in": normal(ks[2], (DEPTH, D_MODEL, IN_DIM), D_MODEL ** -0.5),
        "q_norm_gain": 1.0 + normal(ks[3], (DEPTH, ATTN_HEAD_DIM), 0.1),
        "k_norm_gain": 1.0 + normal(ks[4], (DEPTH, ATTN_HEAD_DIM), 0.1),
        "attn_sinks": normal(ks[5], (DEPTH, ATTN_HEADS), 0.5),
        "rel_bias": normal(ks[6], (NUM_BUCKETS, ATTN_HEADS), 0.5),
        "hgrn_lb_logits": normal(ks[7], (DEPTH, HG_K_DIM), 0.5),
        "hgrn_norm_gain": 1.0 + normal(ks[8], (DEPTH, HG_HEAD_V), 0.1),
        "w_branch": normal(ks[9], (DEPTH, MIX_DIM, D_MODEL), 1.0) * row_scale,
        "w_out": normal(ks[10], (DEPTH, D_MODEL, D_MODEL), D_MODEL ** -0.5),
        "mlp_norm_gain": 1.0 + normal(ks[11], (DEPTH, D_MODEL), 0.1),
        "w_up": normal(ks[12], (DEPTH, D_MODEL, D_FF), D_MODEL ** -0.5),
        "w_down": normal(ks[13], (DEPTH, D_FF, D_MODEL), D_FF ** -0.5),
    }


def reference(x, attn_norm_gain, w_in, q_norm_gain, k_norm_gain, attn_sinks, rel_bias,
              hgrn_lb_logits, hgrn_norm_gain, w_branch, w_out, mlp_norm_gain, w_up, w_down):
    B, S, _ = x.shape
    bias, band_mask = relative_bias_band(rel_bias, S // BLOCK)
    lb_cum = jnp.cumsum(jax.nn.softmax(hgrn_lb_logits.astype(jnp.float32), axis=0), axis=0)
    lower_bounds = lb_cum - lb_cum[:1]
    for l in range(DEPTH):
        h = rms_norm(x, attn_norm_gain[l])
        proj = jnp.einsum('bsd,de->bse', h, w_in[l])
        aq, ak, av, hq, hf, hi, hg, ga, gh = jnp.split(proj, SPLIT_POINTS, axis=-1)
        aq = rms_norm(aq.reshape(B, S, ATTN_HEADS, ATTN_HEAD_DIM), q_norm_gain[l])
        ak = rms_norm(ak.reshape(B, S, ATTN_KV_HEADS, ATTN_HEAD_DIM), k_norm_gain[l])
        av = av.reshape(B, S, ATTN_KV_HEADS, ATTN_HEAD_DIM)
        o_attn = sliding_window_attention(aq, ak, av, attn_sinks[l], bias, band_mask)
        o_hgrn = hgrn2(hq, hf, hi, hg, lower_bounds[l], hgrn_norm_gain[l])
        branch_a = o_attn @ w_branch[l, :ATTN_Q_DIM]
        branch_h = o_hgrn @ w_branch[l, ATTN_Q_DIM:]
        merged = jax.nn.sigmoid(ga) * branch_a + jax.nn.sigmoid(gh) * branch_h
        x = x + merged @ w_out[l]
        h = rms_norm(x, mlp_norm_gain[l])
        x = x + jnp.square(jax.nn.relu(h @ w_up[l])) @ w_down[l]
    return x
```

```python
import functools
import math

import numpy as np
import jax
import jax.numpy as jnp
from jax import lax
from jax.experimental import pallas as pl
from jax.experimental.pallas import tpu as pltpu

ATTN_HEADS = 32
ATTN_KV_HEADS = 4
ATTN_HEAD_DIM = 64
ATTN_GROUP = ATTN_HEADS // ATTN_KV_HEADS
ATTN_BLOCK = 128
ATTN_SCALE = ATTN_HEAD_DIM ** -0.5
NUM_BUCKETS = 32
MAX_DISTANCE = 128
HG_HEADS = 8
HG_DK = 128
HG_DV = 128
HG_CHUNK = 64
EPS = 1e-6

ATTN_Q_DIM = ATTN_HEADS * ATTN_HEAD_DIM
ATTN_KV_DIM = ATTN_KV_HEADS * ATTN_HEAD_DIM
QKV_DIM = ATTN_Q_DIM + 2 * ATTN_KV_DIM
HG_DIM = HG_HEADS * HG_DK
HGRN_IN_DIM = 4 * HG_DIM

V7X_LANES = 128
V7X_VMEM_LIMIT_BYTES = 56 * 1024 * 1024

_BF16 = jnp.bfloat16
_F32 = jnp.float32


def _nt_dot(a, b):
    return lax.dot_general(a, b, (((1,), (1,)), ((), ())), preferred_element_type=_F32)


def _tn_dot(a, b):
    return lax.dot_general(a, b, (((0,), (0,)), ((), ())), preferred_element_type=_F32)


def _dot(a, b):
    return jnp.dot(a, b, preferred_element_type=_F32)


def _split3_bf16(x):
    a1 = x.astype(_BF16)
    r1 = x - a1.astype(_F32)
    a2 = r1.astype(_BF16)
    a3 = (r1 - a2.astype(_F32)).astype(_BF16)
    return a1, a2, a3


def _compiler_params(n_grid):
    return pltpu.CompilerParams(
        dimension_semantics=("arbitrary",) * n_grid,
        vmem_limit_bytes=V7X_VMEM_LIMIT_BYTES,
    )


def _rmsnorm_kernel(x_ref, g_ref, o_ref):
    x = x_ref[...]
    ms = jnp.mean(x * x, axis=-1, keepdims=True)
    o_ref[...] = (x * lax.rsqrt(ms + EPS) * g_ref[...]).astype(o_ref.dtype)


def _rmsnorm(x, gain, *, rows=256):
    m, d = x.shape
    rows = min(rows, m)
    return pl.pallas_call(
        _rmsnorm_kernel,
        grid=(m // rows,),
        in_specs=[pl.BlockSpec((rows, d), lambda i: (i, 0)),
                  pl.BlockSpec((1, d), lambda i: (0, 0))],
        out_specs=pl.BlockSpec((rows, d), lambda i: (i, 0)),
        out_shape=jax.ShapeDtypeStruct((m, d), _BF16),
        compiler_params=_compiler_params(1),
        name="rmsnorm",
    )(x, gain.reshape(1, d).astype(_F32))


def _relu2(y):
    r = jnp.maximum(y, 0.0)
    return r * r


def _mm_kernel(a_ref, w_ref, o_ref, *, epilogue):
    y = _dot(a_ref[...], w_ref[...])
    if epilogue is not None:
        y = epilogue(y)
    o_ref[...] = y.astype(o_ref.dtype)


def _mm_res_kernel(a_ref, w_ref, r_ref, o_ref):
    o_ref[...] = r_ref[...] + _dot(a_ref[...], w_ref[...])


def _matmul(a, w, *, tm, tn, out_dtype, epilogue=None, residual=None, name):
    m, k = a.shape
    _, n = w.shape
    tm, tn = min(tm, m), min(tn, n)
    grid = (m // tm, n // tn)
    in_specs = [pl.BlockSpec((tm, k), lambda i, j: (i, 0)),
                pl.BlockSpec((k, tn), lambda i, j: (0, j))]
    args = [a, w]
    if residual is None:
        body = functools.partial(_mm_kernel, epilogue=epilogue)
    else:
        body = _mm_res_kernel
        in_specs.append(pl.BlockSpec((tm, tn), lambda i, j: (i, j)))
        args.append(residual)
    return pl.pallas_call(
        body,
        grid=grid,
        in_specs=in_specs,
        out_specs=pl.BlockSpec((tm, tn), lambda i, j: (i, j)),
        out_shape=jax.ShapeDtypeStruct((m, n), out_dtype),
        compiler_params=_compiler_params(2),
        name=name,
    )(*args)


def _mm_ktiled_res_kernel(a_ref, w_ref, r_ref, o_ref, acc_ref):
    kk = pl.program_id(2)

    @pl.when(kk == 0)
    def _():
        acc_ref[...] = r_ref[...]

    acc_ref[...] += _dot(a_ref[...], w_ref[...])

    @pl.when(kk == pl.num_programs(2) - 1)
    def _():
        o_ref[...] = acc_ref[...]


def _matmul_ktiled_res(a, w, residual, *, tm, tn, tk, name):
    m, k = a.shape
    _, n = w.shape
    tm, tn, tk = min(tm, m), min(tn, n), min(tk, k)
    return pl.pallas_call(
        _mm_ktiled_res_kernel,
        grid=(m // tm, n // tn, k // tk),
        in_specs=[pl.BlockSpec((tm, tk), lambda i, j, q: (i, q)),
                  pl.BlockSpec((tk, tn), lambda i, j, q: (q, j)),
                  pl.BlockSpec((tm, tn), lambda i, j, q: (i, j))],
        out_specs=pl.BlockSpec((tm, tn), lambda i, j, q: (i, j)),
        out_shape=jax.ShapeDtypeStruct((m, n), _F32),
        scratch_shapes=[pltpu.VMEM((tm, tn), _F32)],
        compiler_params=_compiler_params(3),
        name=name,
    )(a, w, residual)


def _merge_kernel(h_ref, oa_ref, oh_ref, wga_ref, wgh_ref, wa_ref, wh_ref, o_ref):
    h = h_ref[...]
    ga = _dot(h, wga_ref[...])
    gh = _dot(h, wgh_ref[...])
    ba = _dot(oa_ref[...], wa_ref[...])
    bh = _dot(oh_ref[...], wh_ref[...])
    o_ref[...] = (jax.nn.sigmoid(ga) * ba + jax.nn.sigmoid(gh) * bh).astype(o_ref.dtype)


def _merge(h, o_attn, o_hgrn, w_gate, w_branch, *, tm=1024, tn=256):
    m, d = h.shape
    tm, tn = min(tm, m), min(tn, d)
    n_tiles = d // tn
    assert ATTN_Q_DIM % HG_DIM == 0
    return pl.pallas_call(
        _merge_kernel,
        grid=(m // tm, n_tiles),
        in_specs=[
            pl.BlockSpec((tm, d), lambda i, j: (i, 0)),
            pl.BlockSpec((tm, ATTN_Q_DIM), lambda i, j: (i, 0)),
            pl.BlockSpec((tm, HG_DIM), lambda i, j: (i, 0)),
            pl.BlockSpec((d, tn), lambda i, j: (0, j)),
            pl.BlockSpec((d, tn), lambda i, j: (0, j + n_tiles)),
            pl.BlockSpec((ATTN_Q_DIM, tn), lambda i, j: (0, j)),
            pl.BlockSpec((HG_DIM, tn), lambda i, j: (ATTN_Q_DIM // HG_DIM, j)),
        ],
        out_specs=pl.BlockSpec((tm, tn), lambda i, j: (i, j)),
        out_shape=jax.ShapeDtypeStruct((m, d), _BF16),
        compiler_params=_compiler_params(2),
        name="gated_merge",
    )(h, o_attn, o_hgrn, w_gate, w_gate, w_branch, w_branch)


def _bucket_table():
    qi = np.arange(ATTN_BLOCK)[:, None]
    ki = np.arange(2 * ATTN_BLOCK)[None, :]
    dist = qi + ATTN_BLOCK - ki
    in_window = (dist >= 0) & (dist < ATTN_BLOCK)
    dist = np.maximum(dist, 0)
    max_exact = NUM_BUCKETS // 2
    d = np.maximum(dist, 1).astype(np.float64)
    val = np.log(d / max_exact) / math.log(MAX_DISTANCE / max_exact) * (NUM_BUCKETS - max_exact)
    large = max_exact + np.trunc(val).astype(np.int64)
    frac = np.abs(val - np.round(val))[in_window & (dist > max_exact)]
    assert frac.min() > 1e-3
    bucket = np.where(dist < max_exact, dist, np.minimum(large, NUM_BUCKETS - 1))
    return bucket.reshape(-1), in_window.reshape(-1)


def _bias_kernel(rbt_ref, onehot_ref, mask_ref, o_ref):
    a1, a2, a3 = _split3_bf16(rbt_ref[...])
    lhs = jnp.concatenate([a1, a2, a3], axis=1)
    o_ref[...] = _dot(lhs, onehot_ref[...]) + mask_ref[...]


def _bias_band(rel_bias, *, tn=4096):
    bucket, in_window = _bucket_table()
    n = bucket.shape[0]
    onehot = (np.arange(NUM_BUCKETS)[:, None] == bucket[None, :]).astype(np.float32)
    onehot3 = jnp.asarray(np.concatenate([onehot] * 3, axis=0), dtype=_BF16)
    maskadd = jnp.asarray(np.where(in_window, 0.0, -np.inf)[None, :], dtype=_F32)
    out = pl.pallas_call(
        _bias_kernel,
        grid=(n // tn,),
        in_specs=[pl.BlockSpec((ATTN_HEADS, NUM_BUCKETS), lambda j: (0, 0)),
                  pl.BlockSpec((3 * NUM_BUCKETS, tn), lambda j: (0, j)),
                  pl.BlockSpec((1, tn), lambda j: (0, j))],
        out_specs=pl.BlockSpec((ATTN_HEADS, tn), lambda j: (0, j)),
        out_shape=jax.ShapeDtypeStruct((ATTN_HEADS, n), _F32),
        compiler_params=_compiler_params(1),
        name="rel_bias_band",
    )(rel_bias.astype(_F32).T, onehot3, maskadd)
    return out.reshape(ATTN_HEADS, ATTN_BLOCK, 2 * ATTN_BLOCK)


def _group_rms(x, blockdiag, gain):
    x2 = x * x
    hi = x2.astype(_BF16)
    lo = (x2 - hi.astype(_F32)).astype(_BF16)
    ss = _dot(hi, blockdiag) + _dot(lo, blockdiag)
    return x * lax.rsqrt(ss * (1.0 / ATTN_HEAD_DIM) + EPS) * gain


def _pair_stack(slab, low_half):
    swapped = pltpu.roll(slab, ATTN_HEAD_DIM, 1)
    zero = jnp.zeros_like(slab)
    even = jnp.concatenate([jnp.where(low_half, slab, zero), jnp.where(low_half, zero, swapped)], axis=0)
    odd = jnp.concatenate([jnp.where(low_half, swapped, zero), jnp.where(low_half, zero, slab)], axis=0)
    return even.astype(_BF16), odd.astype(_BF16)


def _attn_kernel(sink_ref, q_ref, kvc_ref, kvp_ref, bias_ref, gq_ref, gk_ref, bd_ref, o_ref):
    j = pl.program_id(1)
    w = 2 * ATTN_BLOCK
    bd = bd_ref[...]
    kv = jnp.concatenate([kvp_ref[...], kvc_ref[...]], axis=0)
    kn = _group_rms(kv[:, :ATTN_KV_DIM], bd, gk_ref[...])
    v = kv[:, ATTN_KV_DIM:]

    lane = lax.broadcasted_iota(jnp.int32, (w, V7X_LANES), 1)
    low_half = lane < ATTN_HEAD_DIM
    k_stacks, v_stacks = [], []
    for s in range(ATTN_KV_DIM // V7X_LANES):
        k_stacks.extend(_pair_stack(kn[:, s * V7X_LANES:(s + 1) * V7X_LANES], low_half))
        v_stacks.extend(_pair_stack(v[:, s * V7X_LANES:(s + 1) * V7X_LANES], low_half))

    col = lax.broadcasted_iota(jnp.int32, (ATTN_BLOCK, w), 1)
    keep = jnp.logical_or(col >= ATTN_BLOCK, j > 0)
    neg_inf = jnp.full((ATTN_BLOCK, w), -jnp.inf, _F32)

    for s in range(ATTN_Q_DIM // w):
        q = q_ref[:, s * w:(s + 1) * w]
        qn = (_group_rms(q, bd, gq_ref[...]) * ATTN_SCALE).astype(_BF16)
        g = (s * w) // (ATTN_GROUP * ATTN_HEAD_DIM)
        for half in range(2):
            pair = 2 * s + half
            sc = _nt_dot(qn[:, half * V7X_LANES:(half + 1) * V7X_LANES], k_stacks[g])
            probs = []
            for e in range(2):
                head = 2 * pair + e
                se = sc[:, e * w:(e + 1) * w] + bias_ref[head]
                se = jnp.where(keep, se, neg_inf)
                sink = sink_ref[head]
                mx = jnp.maximum(jnp.max(se, axis=-1, keepdims=True), sink)
                p = jnp.exp(se - mx)
                denom = jnp.sum(p, axis=-1, keepdims=True) + jnp.exp(sink - mx)
                probs.append((p * (1.0 / denom)).astype(_BF16))
            o = _dot(jnp.concatenate(probs, axis=1), v_stacks[g])
            o_ref[:, pair * V7X_LANES:(pair + 1) * V7X_LANES] = o.astype(o_ref.dtype)


def _attention(qkv, sinks, bias, q_gain, k_gain, batch, seq):
    nb = seq // ATTN_BLOCK
    kv_col = ATTN_Q_DIM // (2 * ATTN_KV_DIM)
    assert ATTN_Q_DIM % (2 * ATTN_KV_DIM) == 0
    blockdiag = jnp.asarray(np.kron(np.eye(2 * ATTN_BLOCK // ATTN_HEAD_DIM), np.ones((ATTN_HEAD_DIM,) * 2)), _BF16)
    tile = lambda g: jnp.tile(g.reshape(1, ATTN_HEAD_DIM).astype(_F32), (1, 2 * ATTN_BLOCK // ATTN_HEAD_DIM))
    return pl.pallas_call(
        _attn_kernel,
        grid=(batch, nb),
        in_specs=[
            pl.BlockSpec(memory_space=pltpu.SMEM),
            pl.BlockSpec((ATTN_BLOCK, ATTN_Q_DIM), lambda b, j: (b * nb + j, 0)),
            pl.BlockSpec((ATTN_BLOCK, 2 * ATTN_KV_DIM), lambda b, j: (b * nb + j, kv_col)),
            pl.BlockSpec((ATTN_BLOCK, 2 * ATTN_KV_DIM), lambda b, j: (b * nb + jnp.maximum(j - 1, 0), kv_col)),
            pl.BlockSpec((ATTN_HEADS, ATTN_BLOCK, 2 * ATTN_BLOCK), lambda b, j: (0, 0, 0)),
            pl.BlockSpec((1, 2 * ATTN_BLOCK), lambda b, j: (0, 0)),
            pl.BlockSpec((1, 2 * ATTN_BLOCK), lambda b, j: (0, 0)),
            pl.BlockSpec((2 * ATTN_BLOCK, 2 * ATTN_BLOCK), lambda b, j: (0, 0)),
        ],
        out_specs=pl.BlockSpec((ATTN_BLOCK, ATTN_Q_DIM), lambda b, j: (b * nb + j, 0)),
        out_shape=jax.ShapeDtypeStruct((batch * seq, ATTN_Q_DIM), _BF16),
        compiler_params=_compiler_params(2),
        name="swa_attention",
    )(sinks.astype(_F32), qkv, qkv, qkv, bias, tile(q_gain), tile(k_gain), blockdiag)


_HG_LEVELS = tuple(HG_CHUNK >> (i + 1) for i in range(int(math.log2(HG_CHUNK))))


def _hgrn_tables():
    c = HG_CHUNK
    t = np.arange(c)
    blocks = [(t[None, :] <= t[:, None]), (t[None, :] > t[:, None])]
    masks = [np.eye(c, dtype=bool)]
    for m in _HG_LEVELS:
        ref = (t // (2 * m)) * (2 * m) + m - 1
        upper = (t % (2 * m)) >= m
        up_rows = upper[:, None] & (t[None, :] > ref[:, None]) & (t[None, :] <= t[:, None])
        lo_rows = (~upper)[:, None] & (t[None, :] > t[:, None]) & (t[None, :] <= ref[:, None])
        blocks.append(up_rows | lo_rows)
        same = (t[:, None] // (2 * m)) == (t[None, :] // (2 * m))
        masks.append(same & upper[:, None] & (~upper)[None, :])
    sums = np.concatenate(blocks, axis=0).astype(np.float32)
    return np.concatenate([sums] * 3, axis=1), np.stack(masks).astype(np.float32)


def _hgrn_kernel(p_ref, lbl_ref, gain_ref, sums_ref, masks_ref, o_ref, state_ref, *, layer, rows):
    c = HG_CHUNK

    @pl.when(pl.program_id(1) == 0)
    def _():
        state_ref[...] = jnp.zeros_like(state_ref)

    lg = lbl_ref[...]
    e = jnp.exp(lg - jnp.max(lg, axis=0, keepdims=True))
    sm = e / jnp.sum(e, axis=0, keepdims=True)
    lb_all = jnp.zeros_like(sm[0:1])
    for i in range(1, layer + 1):
        lb_all = lb_all + sm[i:i + 1]

    row = lax.broadcasted_iota(jnp.int32, (c, 1), 0)
    gain = gain_ref[...]
    sums = sums_ref[...]

    def chunk_body(ci, carry):
        r0 = pl.multiple_of(ci * c, c)
        for h in range(HG_HEADS):
            lanes = slice(h * HG_DK, (h + 1) * HG_DK)
            hq = p_ref[pl.ds(r0, c), lanes]
            hf = p_ref[pl.ds(r0, c), HG_DIM + h * HG_DK:HG_DIM + (h + 1) * HG_DK]
            hi = p_ref[pl.ds(r0, c), 2 * HG_DIM + h * HG_DV:2 * HG_DIM + (h + 1) * HG_DV]
            hg = p_ref[pl.ds(r0, c), 3 * HG_DIM + h * HG_DV:3 * HG_DIM + (h + 1) * HG_DV]
            lb = lb_all[:, lanes]
            forget = lb + (1.0 - lb) * jax.nn.sigmoid(hf)
            logf = jnp.log(forget)
            kk = 1.0 - forget
            qf = hq * jax.nn.sigmoid(hq)
            v = hi.astype(_BF16)

            expo = _dot(sums, jnp.concatenate(_split3_bf16(logf), axis=0))
            decay = jnp.exp(expo)
            e_cum, e_end = decay[0:c], decay[c:2 * c]
            state_t = state_ref[h]
            o = _nt_dot((qf * e_cum).astype(_BF16), state_t.astype(_BF16))
            scores = _nt_dot(qf.astype(_BF16), kk.astype(_BF16)) * masks_ref[0]
            for li, m in enumerate(_HG_LEVELS):
                upper = (row & m) != 0
                z = (jnp.where(upper, qf, kk) * decay[(2 + li) * c:(3 + li) * c]).astype(_BF16)
                scores = scores + _nt_dot(z, z) * masks_ref[1 + li]
            o = o + _dot(scores.astype(_BF16), v)
            state_ref[h] = state_t * e_cum[c - 1:c] + _tn_dot(v, (kk * e_end).astype(_BF16))

            y = o * lax.rsqrt(jnp.mean(o * o, axis=-1, keepdims=True) + EPS) * gain
            y = y * (hg * jax.nn.sigmoid(hg))
            o_ref[pl.ds(r0, c), h * HG_DV:(h + 1) * HG_DV] = y.astype(o_ref.dtype)
        return carry

    lax.fori_loop(0, rows // c, chunk_body, 0)


def _hgrn(p, lb_logits, norm_gain, layer, batch, seq, *, rows=256):
    rows = min(rows, seq)
    steps = seq // rows
    depth = lb_logits.shape[0]
    sums, masks = _hgrn_tables()
    return pl.pallas_call(
        functools.partial(_hgrn_kernel, layer=layer, rows=rows),
        grid=(batch, steps),
        in_specs=[
            pl.BlockSpec((rows, HGRN_IN_DIM), lambda b, s: (b * steps + s, 0)),
            pl.BlockSpec((depth, HG_DIM), lambda b, s: (0, 0)),
            pl.BlockSpec((1, HG_DV), lambda b, s: (0, 0)),
            pl.BlockSpec(sums.shape, lambda b, s: (0, 0)),
            pl.BlockSpec(masks.shape, lambda b, s: (0, 0, 0)),
        ],
        out_specs=pl.BlockSpec((rows, HG_DIM), lambda b, s: (b * steps + s, 0)),
        out_shape=jax.ShapeDtypeStruct((batch * seq, HG_DIM), _BF16),
        scratch_shapes=[pltpu.VMEM((HG_HEADS, HG_DV, HG_DK), _F32)],
        compiler_params=_compiler_params(2),
        name="hgrn2_scan",
    )(p, lb_logits.astype(_F32), norm_gain.reshape(1, HG_DV).astype(_F32),
      jnp.asarray(sums, _BF16), jnp.asarray(masks, _F32))


def kernel(x, attn_norm_gain, w_in, q_norm_gain, k_norm_gain, attn_sinks, rel_bias,
           hgrn_lb_logits, hgrn_norm_gain, w_branch, w_out, mlp_norm_gain, w_up, w_down):
    batch, seq, d = x.shape
    depth = w_in.shape[0]
    m = batch * seq
    xf = x.reshape(m, d).astype(_F32)
    bias = _bias_band(rel_bias)
    hg_end = QKV_DIM + HGRN_IN_DIM
    for l in range(depth):
        w_qkv = w_in[l, :, :QKV_DIM].astype(_BF16)
        w_hg = w_in[l, :, QKV_DIM:hg_end].astype(_BF16)
        w_gate = w_in[l, :, hg_end:].astype(_BF16)
        w_br = w_branch[l].astype(_BF16)
        w_o = w_out[l].astype(_BF16)
        w_u = w_up[l].astype(_BF16)
        w_d = w_down[l].astype(_BF16)

        h = _rmsnorm(xf, attn_norm_gain[l])
        qkv = _matmul(h, w_qkv, tm=1024, tn=512, out_dtype=_F32, name="proj_qkv")
        hgp = _matmul(h, w_hg, tm=1024, tn=512, out_dtype=_F32, name="proj_hgrn")
        o_attn = _attention(qkv, attn_sinks[l], bias, q_norm_gain[l], k_norm_gain[l], batch, seq)
        o_hgrn = _hgrn(hgp, hgrn_lb_logits, hgrn_norm_gain[l], l, batch, seq)
        merged = _merge(h, o_attn, o_hgrn, w_gate, w_br)
        xf = _matmul(merged, w_o, tm=1024, tn=512, out_dtype=_F32, residual=xf, name="out_proj")

        h2 = _rmsnorm(xf, mlp_norm_gain[l])
        u = _matmul(h2, w_u, tm=1024, tn=1024, out_dtype=_BF16, epilogue=_relu2, name="mlp_up")
        xf = _matmul_ktiled_res(u, w_d, xf, tm=1024, tn=1024, tk=2048, name="mlp_down")
    return xf.reshape(batch, seq, d).astype(x.dtype)
```

```python
import functools
import math
from typing import NamedTuple

import numpy as np
import jax
import jax.numpy as jnp
from jax import lax
from jax.experimental import pallas as pl
from jax.experimental.pallas import tpu as pltpu

ATTN_HEADS = 32
ATTN_KV_HEADS = 4
ATTN_HEAD_DIM = 64
ATTN_GROUP = ATTN_HEADS // ATTN_KV_HEADS
ATTN_BLOCK = 128
ATTN_SCALE = ATTN_HEAD_DIM ** -0.5
NUM_BUCKETS = 32
MAX_DISTANCE = 128
HG_HEADS = 8
HG_DK = 128
HG_DV = 128
HG_CHUNK = 64
EPS = 1e-6

ATTN_Q_DIM = ATTN_HEADS * ATTN_HEAD_DIM
ATTN_KV_DIM = ATTN_KV_HEADS * ATTN_HEAD_DIM
QKV_DIM = ATTN_Q_DIM + 2 * ATTN_KV_DIM
HG_DIM = HG_HEADS * HG_DK
HGRN_IN_DIM = 4 * HG_DIM

V7X_LANES = 128
BF16_SUBLANES = 16
V7X_VMEM_LIMIT_BYTES = 56 * 1024 * 1024

_BF16 = jnp.bfloat16
_F32 = jnp.float32


class _Tiles(NamedTuple):
    tm: int = 1024
    proj_tn: int = 512
    merge_tn: int = 256
    up_tn: int = 1024
    down_tn: int = 1024
    down_tk: int = 2048


_TILES = _Tiles()


def _nt_dot(a, b):
    return lax.dot_general(a, b, (((1,), (1,)), ((), ())), preferred_element_type=_F32)


def _tn_dot(a, b):
    return lax.dot_general(a, b, (((0,), (0,)), ((), ())), preferred_element_type=_F32)


def _dot(a, b):
    return jnp.dot(a, b, preferred_element_type=_F32)


def _split3_bf16(x):
    a1 = x.astype(_BF16)
    r1 = x - a1.astype(_F32)
    a2 = r1.astype(_BF16)
    a3 = (r1 - a2.astype(_F32)).astype(_BF16)
    return a1, a2, a3


def _compiler_params(n_grid):
    return pltpu.CompilerParams(
        dimension_semantics=("arbitrary",) * n_grid,
        vmem_limit_bytes=V7X_VMEM_LIMIT_BYTES,
    )


def _rmsnorm_kernel(x_ref, g_ref, o_ref):
    x = x_ref[...]
    ms = jnp.mean(x * x, axis=-1, keepdims=True)
    o_ref[...] = (x * lax.rsqrt(ms + EPS) * g_ref[...]).astype(o_ref.dtype)


def _rmsnorm(x, gain, *, rows=256):
    m, d = x.shape
    rows = min(rows, m)
    return pl.pallas_call(
        _rmsnorm_kernel,
        grid=(m // rows,),
        in_specs=[pl.BlockSpec((rows, d), lambda i: (i, 0)),
                  pl.BlockSpec((1, d), lambda i: (0, 0))],
        out_specs=pl.BlockSpec((rows, d), lambda i: (i, 0)),
        out_shape=jax.ShapeDtypeStruct((m, d), _BF16),
        compiler_params=_compiler_params(1),
        name="rmsnorm",
    )(x, gain.reshape(1, d).astype(_F32))


class _Cvt(NamedTuple):
    stacked: jax.Array
    layer: int


def _cvt_plumbing(cvts, n_steps, linear_step):
    in_specs, out_specs, out_shapes, args = [], [], [], []
    for c in cvts:
        _, rows, cols = c.stacked.shape
        units = rows // BF16_SUBLANES
        assert rows % BF16_SUBLANES == 0
        blocks = max(b for b in range(1, min(units, n_steps) + 1) if units % b == 0)
        brows = rows // blocks
        idx = functools.partial(lambda *g, blocks: (linear_step(*g) * blocks) // n_steps, blocks=blocks)
        in_specs.append(pl.BlockSpec((None, brows, cols),
                                     functools.partial(lambda *g, idx, layer: (layer, idx(*g), 0), idx=idx, layer=c.layer)))
        out_specs.append(pl.BlockSpec((brows, cols), functools.partial(lambda *g, idx: (idx(*g), 0), idx=idx)))
        out_shapes.append(jax.ShapeDtypeStruct((rows, cols), _BF16))
        args.append(c.stacked)
    return in_specs, out_specs, out_shapes, args


def _run_cvts(src_refs, dst_refs):
    for s, d in zip(src_refs, dst_refs):
        d[...] = s[...].astype(d.dtype)


def _relu2(y):
    r = jnp.maximum(y, 0.0)
    return r * r


def _mm_kernel(*refs, epilogue, has_residual, n_cvt):
    a_ref, w_ref = refs[0], refs[1]
    n_in = 2 + has_residual
    cvt_src = refs[n_in:n_in + n_cvt]
    o_ref = refs[n_in + n_cvt]
    cvt_dst = refs[n_in + n_cvt + 1:]
    y = _dot(a_ref[...], w_ref[...])
    if epilogue is not None:
        y = epilogue(y)
    if has_residual:
        y = refs[2][...] + y
    o_ref[...] = y.astype(o_ref.dtype)
    _run_cvts(cvt_src, cvt_dst)


def _matmul(a, w, *, n_out, col_block0=0, tm, tn, out_dtype, epilogue=None, residual=None, cvts=(), name):
    m, k = a.shape
    tm, tn = min(tm, m), min(tn, n_out)
    grid = (m // tm, n_out // tn)
    n_steps = grid[0] * grid[1]
    in_specs = [pl.BlockSpec((tm, k), lambda i, j: (i, 0)),
                pl.BlockSpec((k, tn), lambda i, j: (0, j + col_block0))]
    args = [a, w]
    if residual is not None:
        in_specs.append(pl.BlockSpec((tm, tn), lambda i, j: (i, j)))
        args.append(residual)
    c_in, c_out, c_shapes, c_args = _cvt_plumbing(cvts, n_steps, lambda i, j: i * grid[1] + j)
    outs = pl.pallas_call(
        functools.partial(_mm_kernel, epilogue=epilogue, has_residual=residual is not None, n_cvt=len(cvts)),
        grid=grid,
        in_specs=in_specs + c_in,
        out_specs=[pl.BlockSpec((tm, tn), lambda i, j: (i, j))] + c_out,
        out_shape=[jax.ShapeDtypeStruct((m, n_out), out_dtype)] + c_shapes,
        compiler_params=_compiler_params(2),
        name=name,
    )(*args, *c_args)
    return outs


def _mm_ktiled_res_kernel(*refs, n_cvt):
    a_ref, w_ref, r_ref = refs[:3]
    cvt_src = refs[3:3 + n_cvt]
    o_ref = refs[3 + n_cvt]
    cvt_dst = refs[4 + n_cvt:4 + 2 * n_cvt]
    acc_ref = refs[4 + 2 * n_cvt]
    kk = pl.program_id(2)

    @pl.when(kk == 0)
    def _():
        acc_ref[...] = r_ref[...]

    acc_ref[...] += _dot(a_ref[...], w_ref[...])
    _run_cvts(cvt_src, cvt_dst)

    @pl.when(kk == pl.num_programs(2) - 1)
    def _():
        o_ref[...] = acc_ref[...]


def _matmul_ktiled_res(a, w, residual, *, tm, tn, tk, cvts=(), name):
    m, k = a.shape
    _, n = w.shape
    tm, tn, tk = min(tm, m), min(tn, n), min(tk, k)
    grid = (m // tm, n // tn, k // tk)
    n_steps = grid[0] * grid[1] * grid[2]
    c_in, c_out, c_shapes, c_args = _cvt_plumbing(
        cvts, n_steps, lambda i, j, q: (i * grid[1] + j) * grid[2] + q)
    return pl.pallas_call(
        functools.partial(_mm_ktiled_res_kernel, n_cvt=len(cvts)),
        grid=grid,
        in_specs=[pl.BlockSpec((tm, tk), lambda i, j, q: (i, q)),
                  pl.BlockSpec((tk, tn), lambda i, j, q: (q, j)),
                  pl.BlockSpec((tm, tn), lambda i, j, q: (i, j))] + c_in,
        out_specs=[pl.BlockSpec((tm, tn), lambda i, j, q: (i, j))] + c_out,
        out_shape=[jax.ShapeDtypeStruct((m, n), _F32)] + c_shapes,
        scratch_shapes=[pltpu.VMEM((tm, tn), _F32)],
        compiler_params=_compiler_params(3),
        name=name,
    )(a, w, residual, *c_args)


def _merge_kernel(*refs, n_cvt):
    h_ref, oa_ref, oh_ref, wga_ref, wgh_ref, wa_ref, wh_ref = refs[:7]
    cvt_src = refs[7:7 + n_cvt]
    o_ref = refs[7 + n_cvt]
    cvt_dst = refs[8 + n_cvt:]
    h = h_ref[...]
    ga = _dot(h, wga_ref[...])
    gh = _dot(h, wgh_ref[...])
    ba = _dot(oa_ref[...], wa_ref[...])
    bh = _dot(oh_ref[...], wh_ref[...])
    o_ref[...] = (jax.nn.sigmoid(ga) * ba + jax.nn.sigmoid(gh) * bh).astype(o_ref.dtype)
    _run_cvts(cvt_src, cvt_dst)


def _merge(h, o_attn, o_hgrn, w_in_b, w_branch_b, *, gate_col0, tm, tn, cvts=()):
    m, d = h.shape
    tm, tn = min(tm, m), min(tn, d)
    n_tiles = d // tn
    assert ATTN_Q_DIM % HG_DIM == 0 and gate_col0 % tn == 0
    g0 = gate_col0 // tn
    grid = (m // tm, n_tiles)
    c_in, c_out, c_shapes, c_args = _cvt_plumbing(cvts, grid[0] * grid[1], lambda i, j: i * n_tiles + j)
    return pl.pallas_call(
        functools.partial(_merge_kernel, n_cvt=len(cvts)),
        grid=grid,
        in_specs=[
            pl.BlockSpec((tm, d), lambda i, j: (i, 0)),
            pl.BlockSpec((tm, ATTN_Q_DIM), lambda i, j: (i, 0)),
            pl.BlockSpec((tm, HG_DIM), lambda i, j: (i, 0)),
            pl.BlockSpec((d, tn), lambda i, j: (0, g0 + j)),
            pl.BlockSpec((d, tn), lambda i, j: (0, g0 + n_tiles + j)),
            pl.BlockSpec((ATTN_Q_DIM, tn), lambda i, j: (0, j)),
            pl.BlockSpec((HG_DIM, tn), lambda i, j: (ATTN_Q_DIM // HG_DIM, j)),
        ] + c_in,
        out_specs=[pl.BlockSpec((tm, tn), lambda i, j: (i, j))] + c_out,
        out_shape=[jax.ShapeDtypeStruct((m, d), _BF16)] + c_shapes,
        compiler_params=_compiler_params(2),
        name="gated_merge",
    )(h, o_attn, o_hgrn, w_in_b, w_in_b, w_branch_b, w_branch_b, *c_args)


def _bucket_table():
    ki = np.arange(2 * ATTN_BLOCK)[:, None]
    qi = np.arange(ATTN_BLOCK)[None, :]
    dist = qi + ATTN_BLOCK - ki
    in_window = (dist >= 0) & (dist < ATTN_BLOCK)
    in_window_first = in_window & (ki >= ATTN_BLOCK)
    dist = np.maximum(dist, 0)
    max_exact = NUM_BUCKETS // 2
    d = np.maximum(dist, 1).astype(np.float64)
    val = np.log(d / max_exact) / math.log(MAX_DISTANCE / max_exact) * (NUM_BUCKETS - max_exact)
    large = max_exact + np.trunc(val).astype(np.int64)
    frac = np.abs(val - np.round(val))[in_window & (dist > max_exact)]
    assert frac.min() > 1e-3
    bucket = np.where(dist < max_exact, dist, np.minimum(large, NUM_BUCKETS - 1))
    return bucket.reshape(-1), in_window_first.reshape(-1), in_window.reshape(-1)


def _bias_kernel(rbt_ref, onehot_ref, mask_ref, o_ref):
    a1, a2, a3 = _split3_bf16(rbt_ref[...])
    lhs = jnp.concatenate([a1, a2, a3], axis=1)
    base = _dot(lhs, onehot_ref[...])
    o_ref[0] = base + mask_ref[0]
    o_ref[1] = base + mask_ref[1]


def _bias_band(rel_bias, *, tn=4096):
    bucket, first, other = _bucket_table()
    n = bucket.shape[0]
    onehot = (np.arange(NUM_BUCKETS)[:, None] == bucket[None, :]).astype(np.float32)
    onehot3 = jnp.asarray(np.concatenate([onehot] * 3, axis=0), dtype=_BF16)
    maskadd = jnp.asarray(np.where(np.stack([first, other])[:, None, :], 0.0, -np.inf), dtype=_F32)
    out = pl.pallas_call(
        _bias_kernel,
        grid=(n // tn,),
        in_specs=[pl.BlockSpec((ATTN_HEADS, NUM_BUCKETS), lambda j: (0, 0)),
                  pl.BlockSpec((3 * NUM_BUCKETS, tn), lambda j: (0, j)),
                  pl.BlockSpec((2, 1, tn), lambda j: (0, 0, j))],
        out_specs=pl.BlockSpec((2, ATTN_HEADS, tn), lambda j: (0, 0, j)),
        out_shape=jax.ShapeDtypeStruct((2, ATTN_HEADS, n), _F32),
        compiler_params=_compiler_params(1),
        name="rel_bias_band",
    )(rel_bias.astype(_F32).T, onehot3, maskadd)
    return out.reshape(2, ATTN_HEADS, 2 * ATTN_BLOCK, ATTN_BLOCK)


def _group_rms(x, blockdiag, gain):
    x2 = x * x
    hi = x2.astype(_BF16)
    lo = (x2 - hi.astype(_F32)).astype(_BF16)
    ss = _dot(hi, blockdiag) + _dot(lo, blockdiag)
    return x * lax.rsqrt(ss * (1.0 / ATTN_HEAD_DIM) + EPS) * gain


def _pair_stack(slab, low_half):
    swapped = pltpu.roll(slab, ATTN_HEAD_DIM, 1)
    zero = jnp.zeros_like(slab)
    even = jnp.concatenate([jnp.where(low_half, slab, zero), jnp.where(low_half, zero, swapped)], axis=0)
    odd = jnp.concatenate([jnp.where(low_half, swapped, zero), jnp.where(low_half, zero, slab)], axis=0)
    return even.astype(_BF16), odd.astype(_BF16)


def _attn_kernel(sink_ref, q_ref, kvc_ref, kvp_ref, bias_ref, gq_ref, gk_ref, bd_ref, o_ref):
    w = 2 * ATTN_BLOCK
    bd = bd_ref[...]
    kv = jnp.concatenate([kvp_ref[...], kvc_ref[...]], axis=0)
    kn = _group_rms(kv[:, :ATTN_KV_DIM], bd, gk_ref[...])
    v = kv[:, ATTN_KV_DIM:]

    lane = lax.broadcasted_iota(jnp.int32, (w, V7X_LANES), 1)
    low_half = lane < ATTN_HEAD_DIM
    k_stacks, v_stacks = [], []
    for s in range(ATTN_KV_DIM // V7X_LANES):
        k_stacks.extend(_pair_stack(kn[:, s * V7X_LANES:(s + 1) * V7X_LANES], low_half))
        v_stacks.extend(_pair_stack(v[:, s * V7X_LANES:(s + 1) * V7X_LANES], low_half))

    for s in range(ATTN_Q_DIM // w):
        q = q_ref[:, s * w:(s + 1) * w]
        qn = (_group_rms(q, bd, gq_ref[...]) * ATTN_SCALE).astype(_BF16)
        g = (s * w) // (ATTN_GROUP * ATTN_HEAD_DIM)
        for half in range(2):
            pair = 2 * s + half
            sc = _nt_dot(k_stacks[g], qn[:, half * V7X_LANES:(half + 1) * V7X_LANES])
            probs = []
            for e in range(2):
                head = 2 * pair + e
                se = sc[e * w:(e + 1) * w] + bias_ref[head]
                sink = sink_ref[head]
                mx = jnp.maximum(jnp.max(se, axis=0, keepdims=True), sink)
                p = jnp.exp(se - mx)
                denom = jnp.sum(p, axis=0, keepdims=True) + jnp.exp(sink - mx)
                probs.append((p * (1.0 / denom)).astype(_BF16))
            o = _tn_dot(jnp.concatenate(probs, axis=0), v_stacks[g])
            o_ref[:, pair * V7X_LANES:(pair + 1) * V7X_LANES] = o.astype(o_ref.dtype)


def _attention(qkv, sinks, bias, q_gain, k_gain, batch, seq):
    nb = seq // ATTN_BLOCK
    kv_col = ATTN_Q_DIM // (2 * ATTN_KV_DIM)
    assert ATTN_Q_DIM % (2 * ATTN_KV_DIM) == 0
    blockdiag = jnp.asarray(np.kron(np.eye(2 * ATTN_BLOCK // ATTN_HEAD_DIM), np.ones((ATTN_HEAD_DIM,) * 2)), _BF16)
    tile = lambda g: jnp.tile(g.reshape(1, ATTN_HEAD_DIM).astype(_F32), (1, 2 * ATTN_BLOCK // ATTN_HEAD_DIM))
    return pl.pallas_call(
        _attn_kernel,
        grid=(batch, nb),
        in_specs=[
            pl.BlockSpec(memory_space=pltpu.SMEM),
            pl.BlockSpec((ATTN_BLOCK, ATTN_Q_DIM), lambda b, j: (b * nb + j, 0)),
            pl.BlockSpec((ATTN_BLOCK, 2 * ATTN_KV_DIM), lambda b, j: (b * nb + j, kv_col)),
            pl.BlockSpec((ATTN_BLOCK, 2 * ATTN_KV_DIM), lambda b, j: (b * nb + jnp.maximum(j - 1, 0), kv_col)),
            pl.BlockSpec((None, ATTN_HEADS, 2 * ATTN_BLOCK, ATTN_BLOCK),
                         lambda b, j: (jnp.minimum(j, 1), 0, 0, 0)),
            pl.BlockSpec((1, 2 * ATTN_BLOCK), lambda b, j: (0, 0)),
            pl.BlockSpec((1, 2 * ATTN_BLOCK), lambda b, j: (0, 0)),
            pl.BlockSpec((2 * ATTN_BLOCK, 2 * ATTN_BLOCK), lambda b, j: (0, 0)),
        ],
        out_specs=pl.BlockSpec((ATTN_BLOCK, ATTN_Q_DIM), lambda b, j: (b * nb + j, 0)),
        out_shape=jax.ShapeDtypeStruct((batch * seq, ATTN_Q_DIM), _BF16),
        compiler_params=_compiler_params(2),
        name="swa_attention",
    )(sinks.astype(_F32), qkv, qkv, qkv, bias, tile(q_gain), tile(k_gain), blockdiag)


_HG_LEVELS = tuple(HG_CHUNK >> (i + 1) for i in range(int(math.log2(HG_CHUNK))))


def _hgrn_tables():
    c = HG_CHUNK
    t = np.arange(c)
    blocks = [(t[None, :] <= t[:, None]), (t[None, :] > t[:, None])]
    masks = [np.eye(c, dtype=bool)]
    for m in _HG_LEVELS:
        ref = (t // (2 * m)) * (2 * m) + m - 1
        upper = (t % (2 * m)) >= m
        up_rows = upper[:, None] & (t[None, :] > ref[:, None]) & (t[None, :] <= t[:, None])
        lo_rows = (~upper)[:, None] & (t[None, :] > t[:, None]) & (t[None, :] <= ref[:, None])
        blocks.append(up_rows | lo_rows)
        same = (t[:, None] // (2 * m)) == (t[None, :] // (2 * m))
        masks.append(same & upper[:, None] & (~upper)[None, :])
    sums = np.concatenate(blocks, axis=0).astype(np.float32)
    return np.concatenate([sums] * 3, axis=1), np.stack(masks).astype(np.float32)


def _hgrn_kernel(p_ref, lbl_ref, gain_ref, sums_ref, masks_ref, o_ref, state_ref, *, layer, rows):
    c = HG_CHUNK

    @pl.when(pl.program_id(1) == 0)
    def _():
        state_ref[...] = jnp.zeros_like(state_ref)

    lg = lbl_ref[...]
    e = jnp.exp(lg - jnp.max(lg, axis=0, keepdims=True))
    sm = e / jnp.sum(e, axis=0, keepdims=True)
    lb_all = jnp.zeros_like(sm[0:1])
    for i in range(1, layer + 1):
        lb_all = lb_all + sm[i:i + 1]

    row = lax.broadcasted_iota(jnp.int32, (c, 1), 0)
    gain = gain_ref[...]
    sums = sums_ref[...]

    def chunk_body(ci, carry):
        r0 = pl.multiple_of(ci * c, c)
        for h in range(HG_HEADS):
            lanes = slice(h * HG_DK, (h + 1) * HG_DK)
            hq = p_ref[pl.ds(r0, c), lanes]
            hf = p_ref[pl.ds(r0, c), HG_DIM + h * HG_DK:HG_DIM + (h + 1) * HG_DK]
            hi = p_ref[pl.ds(r0, c), 2 * HG_DIM + h * HG_DV:2 * HG_DIM + (h + 1) * HG_DV]
            hg = p_ref[pl.ds(r0, c), 3 * HG_DIM + h * HG_DV:3 * HG_DIM + (h + 1) * HG_DV]
            lb = lb_all[:, lanes]
            forget = lb + (1.0 - lb) * jax.nn.sigmoid(hf)
            logf = jnp.log(forget)
            kk = 1.0 - forget
            qf = hq * jax.nn.sigmoid(hq)
            v = hi.astype(_BF16)

            expo = _dot(sums, jnp.concatenate(_split3_bf16(logf), axis=0))
            decay = jnp.exp(expo)
            e_cum, e_end = decay[0:c], decay[c:2 * c]
            state_t = state_ref[h]
            o = _nt_dot((qf * e_cum).astype(_BF16), state_t.astype(_BF16))
            scores = _nt_dot(qf.astype(_BF16), kk.astype(_BF16)) * masks_ref[0]
            for li, m in enumerate(_HG_LEVELS):
                upper = (row & m) != 0
                z = (jnp.where(upper, qf, kk) * decay[(2 + li) * c:(3 + li) * c]).astype(_BF16)
                scores = scores + _nt_dot(z, z) * masks_ref[1 + li]
            o = o + _dot(scores.astype(_BF16), v)
            state_ref[h] = state_t * e_cum[c - 1:c] + _tn_dot(v, (kk * e_end).astype(_BF16))

            y = o * lax.rsqrt(jnp.mean(o * o, axis=-1, keepdims=True) + EPS) * gain
            y = y * (hg * jax.nn.sigmoid(hg))
            o_ref[pl.ds(r0, c), h * HG_DV:(h + 1) * HG_DV] = y.astype(o_ref.dtype)
        return carry

    lax.fori_loop(0, rows // c, chunk_body, 0)


def _hgrn(p, lb_logits, norm_gain, layer, batch, seq, *, rows=256):
    rows = min(rows, seq)
    steps = seq // rows
    depth = lb_logits.shape[0]
    sums, masks = _hgrn_tables()
    return pl.pallas_call(
        functools.partial(_hgrn_kernel, layer=layer, rows=rows),
        grid=(batch, steps),
        in_specs=[
            pl.BlockSpec((rows, HGRN_IN_DIM), lambda b, s: (b * steps + s, 0)),
            pl.BlockSpec((depth, HG_DIM), lambda b, s: (0, 0)),
            pl.BlockSpec((1, HG_DV), lambda b, s: (0, 0)),
            pl.BlockSpec(sums.shape, lambda b, s: (0, 0)),
            pl.BlockSpec(masks.shape, lambda b, s: (0, 0, 0)),
        ],
        out_specs=pl.BlockSpec((rows, HG_DIM), lambda b, s: (b * steps + s, 0)),
        out_shape=jax.ShapeDtypeStruct((batch * seq, HG_DIM), _BF16),
        scratch_shapes=[pltpu.VMEM((HG_HEADS, HG_DV, HG_DK), _F32)],
        compiler_params=_compiler_params(2),
        name="hgrn2_scan",
    )(p, lb_logits.astype(_F32), norm_gain.reshape(1, HG_DV).astype(_F32),
      jnp.asarray(sums, _BF16), jnp.asarray(masks, _F32))


def kernel(x, attn_norm_gain, w_in, q_norm_gain, k_norm_gain, attn_sinks, rel_bias,
           hgrn_lb_logits, hgrn_norm_gain, w_branch, w_out, mlp_norm_gain, w_up, w_down):
    batch, seq, d = x.shape
    depth = w_in.shape[0]
    m = batch * seq
    xf = x.reshape(m, d).astype(_F32)
    bias = _bias_band(rel_bias)
    t = _TILES
    hg_col0 = QKV_DIM // t.proj_tn
    gate_col0 = QKV_DIM + HGRN_IN_DIM
    assert QKV_DIM % t.proj_tn == 0

    w_in_b = w_in[0].astype(_BF16)
    w_br_b = w_branch[0].astype(_BF16)
    w_o_b = w_out[0].astype(_BF16)
    for l in range(depth):
        h = _rmsnorm(xf, attn_norm_gain[l])
        qkv, = _matmul(h, w_in_b, n_out=QKV_DIM, tm=t.tm, tn=t.proj_tn, out_dtype=_F32, name="proj_qkv")
        hgp, = _matmul(h, w_in_b, n_out=HGRN_IN_DIM, col_block0=hg_col0, tm=t.tm, tn=t.proj_tn,
                       out_dtype=_F32, name="proj_hgrn")
        o_attn = _attention(qkv, attn_sinks[l], bias, q_norm_gain[l], k_norm_gain[l], batch, seq)
        o_hgrn = _hgrn(hgp, hgrn_lb_logits, hgrn_norm_gain[l], l, batch, seq)
        merged, w_u_b = _merge(h, o_attn, o_hgrn, w_in_b, w_br_b, gate_col0=gate_col0,
                               tm=t.tm, tn=t.merge_tn, cvts=(_Cvt(w_up, l),))
        xf, = _matmul(merged, w_o_b, n_out=d, tm=t.tm, tn=t.proj_tn, out_dtype=_F32, residual=xf,
                      name="out_proj")

        h2 = _rmsnorm(xf, mlp_norm_gain[l])
        u, w_d_b = _matmul(h2, w_u_b, n_out=w_up.shape[2], tm=t.tm, tn=t.up_tn, out_dtype=_BF16,
                           epilogue=_relu2, cvts=(_Cvt(w_down, l),), name="mlp_up")
        nxt = () if l + 1 == depth else (_Cvt(w_in, l + 1), _Cvt(w_branch, l + 1), _Cvt(w_out, l + 1))
        xf, *nxt_b = _matmul_ktiled_res(u, w_d_b, xf, tm=t.tm, tn=t.down_tn, tk=t.down_tk, cvts=nxt,
                                        name="mlp_down")
        if nxt_b:
            w_in_b, w_br_b, w_o_b = nxt_b
    return xf.reshape(batch, seq, d).astype(x.dtype)
```

```python
import functools
import math
from typing import Callable, NamedTuple

import numpy as np
import jax
import jax.numpy as jnp
from jax import lax
from jax.experimental import pallas as pl
from jax.experimental.pallas import tpu as pltpu

ATTN_HEADS = 32
ATTN_KV_HEADS = 4
ATTN_HEAD_DIM = 64
ATTN_GROUP = ATTN_HEADS // ATTN_KV_HEADS
ATTN_BLOCK = 128
ATTN_SCALE = ATTN_HEAD_DIM ** -0.5
NUM_BUCKETS = 32
MAX_DISTANCE = 128
HG_HEADS = 8
HG_DK = 128
HG_DV = 128
HG_CHUNK = 64
EPS = 1e-6

ATTN_Q_DIM = ATTN_HEADS * ATTN_HEAD_DIM
ATTN_KV_DIM = ATTN_KV_HEADS * ATTN_HEAD_DIM
QKV_DIM = ATTN_Q_DIM + 2 * ATTN_KV_DIM
HG_DIM = HG_HEADS * HG_DK
HGRN_IN_DIM = 4 * HG_DIM

V7X_LANES = 128
BF16_SUBLANES = 16
V7X_VMEM_LIMIT_BYTES = 56 * 1024 * 1024

_BF16 = jnp.bfloat16
_F32 = jnp.float32


class _Tiles(NamedTuple):
    tm: int = 1024
    proj_tn: int = 512
    merge_tn: int = 256
    up_tn: int = 1024
    down_tn: int = 1024
    down_tk: int = 2048


_TILES = _Tiles()


def _nt_dot(a, b):
    return lax.dot_general(a, b, (((1,), (1,)), ((), ())), preferred_element_type=_F32)


def _tn_dot(a, b):
    return lax.dot_general(a, b, (((0,), (0,)), ((), ())), preferred_element_type=_F32)


def _dot(a, b):
    return jnp.dot(a, b, preferred_element_type=_F32)


def _split3_bf16(x):
    a1 = x.astype(_BF16)
    r1 = x - a1.astype(_F32)
    a2 = r1.astype(_BF16)
    a3 = (r1 - a2.astype(_F32)).astype(_BF16)
    return a1, a2, a3


def _compiler_params(n_grid):
    return pltpu.CompilerParams(
        dimension_semantics=("arbitrary",) * n_grid,
        vmem_limit_bytes=V7X_VMEM_LIMIT_BYTES,
    )


def _rmsnorm_kernel(x_ref, g_ref, o_ref):
    x = x_ref[...]
    ms = jnp.mean(x * x, axis=-1, keepdims=True)
    o_ref[...] = (x * lax.rsqrt(ms + EPS) * g_ref[...]).astype(o_ref.dtype)


def _rmsnorm(x, gain, *, rows=256):
    m, d = x.shape
    rows = min(rows, m)
    return pl.pallas_call(
        _rmsnorm_kernel,
        grid=(m // rows,),
        in_specs=[pl.BlockSpec((rows, d), lambda i: (i, 0)),
                  pl.BlockSpec((1, d), lambda i: (0, 0))],
        out_specs=pl.BlockSpec((rows, d), lambda i: (i, 0)),
        out_shape=jax.ShapeDtypeStruct((m, d), _BF16),
        compiler_params=_compiler_params(1),
        name="rmsnorm",
    )(x, gain.reshape(1, d).astype(_F32))


class _Cvt(NamedTuple):
    stacked: jax.Array
    layer: int


def _cvt_plumbing(cvts, n_steps, linear_step):
    in_specs, out_specs, out_shapes, args = [], [], [], []
    for c in cvts:
        _, rows, cols = c.stacked.shape
        units = rows // BF16_SUBLANES
        assert rows % BF16_SUBLANES == 0
        blocks = max(b for b in range(1, min(units, n_steps) + 1) if units % b == 0)
        brows = rows // blocks
        idx = functools.partial(lambda *g, blocks: (linear_step(*g) * blocks) // n_steps, blocks=blocks)
        in_specs.append(pl.BlockSpec((None, brows, cols),
                                     functools.partial(lambda *g, idx, layer: (layer, idx(*g), 0), idx=idx, layer=c.layer)))
        out_specs.append(pl.BlockSpec((brows, cols), functools.partial(lambda *g, idx: (idx(*g), 0), idx=idx)))
        out_shapes.append(jax.ShapeDtypeStruct((rows, cols), _BF16))
        args.append(c.stacked)
    return in_specs, out_specs, out_shapes, args


def _run_cvts(src_refs, dst_refs):
    for s, d in zip(src_refs, dst_refs):
        d[...] = s[...].astype(d.dtype)


def _relu2(y):
    r = jnp.maximum(y, 0.0)
    return r * r


def _mm_kernel(*refs, epilogue, has_residual, n_cvt):
    a_ref, w_ref = refs[0], refs[1]
    n_in = 2 + has_residual
    cvt_src = refs[n_in:n_in + n_cvt]
    o_ref = refs[n_in + n_cvt]
    cvt_dst = refs[n_in + n_cvt + 1:]
    y = _dot(a_ref[...], w_ref[...])
    if epilogue is not None:
        y = epilogue(y)
    if has_residual:
        y = refs[2][...] + y
    o_ref[...] = y.astype(o_ref.dtype)
    _run_cvts(cvt_src, cvt_dst)


def _matmul(a, w, *, n_out, col_block0=0, tm, tn, out_dtype, epilogue=None, residual=None, cvts=(), name):
    m, k = a.shape
    tm, tn = min(tm, m), min(tn, n_out)
    grid = (m // tm, n_out // tn)
    n_steps = grid[0] * grid[1]
    in_specs = [pl.BlockSpec((tm, k), lambda i, j: (i, 0)),
                pl.BlockSpec((k, tn), lambda i, j: (0, j + col_block0))]
    args = [a, w]
    if residual is not None:
        in_specs.append(pl.BlockSpec((tm, tn), lambda i, j: (i, j)))
        args.append(residual)
    c_in, c_out, c_shapes, c_args = _cvt_plumbing(cvts, n_steps, lambda i, j: i * grid[1] + j)
    outs = pl.pallas_call(
        functools.partial(_mm_kernel, epilogue=epilogue, has_residual=residual is not None, n_cvt=len(cvts)),
        grid=grid,
        in_specs=in_specs + c_in,
        out_specs=[pl.BlockSpec((tm, tn), lambda i, j: (i, j))] + c_out,
        out_shape=[jax.ShapeDtypeStruct((m, n_out), out_dtype)] + c_shapes,
        compiler_params=_compiler_params(2),
        name=name,
    )(*args, *c_args)
    return outs


def _mm_ktiled_res_kernel(*refs, n_cvt):
    a_ref, w_ref, r_ref = refs[:3]
    cvt_src = refs[3:3 + n_cvt]
    o_ref = refs[3 + n_cvt]
    cvt_dst = refs[4 + n_cvt:4 + 2 * n_cvt]

    @pl.when(pl.program_id(2) == 0)
    def _():
        o_ref[...] = r_ref[...]

    o_ref[...] += _dot(a_ref[...], w_ref[...])
    _run_cvts(cvt_src, cvt_dst)


def _matmul_ktiled_res(a, w, residual, *, tm, tn, tk, cvts=(), name):
    m, k = a.shape
    _, n = w.shape
    tm, tn, tk = min(tm, m), min(tn, n), min(tk, k)
    grid = (m // tm, n // tn, k // tk)
    n_steps = grid[0] * grid[1] * grid[2]
    c_in, c_out, c_shapes, c_args = _cvt_plumbing(
        cvts, n_steps, lambda i, j, q: (i * grid[1] + j) * grid[2] + q)
    return pl.pallas_call(
        functools.partial(_mm_ktiled_res_kernel, n_cvt=len(cvts)),
        grid=grid,
        in_specs=[pl.BlockSpec((tm, tk), lambda i, j, q: (i, q)),
                  pl.BlockSpec((tk, tn), lambda i, j, q: (q, j)),
                  pl.BlockSpec((tm, tn), lambda i, j, q: (i, j))] + c_in,
        out_specs=[pl.BlockSpec((tm, tn), lambda i, j, q: (i, j))] + c_out,
        out_shape=[jax.ShapeDtypeStruct((m, n), _F32)] + c_shapes,
        compiler_params=_compiler_params(3),
        name=name,
    )(a, w, residual, *c_args)


def _merge_kernel(*refs, n_cvt):
    h_ref, oa_ref, oh_ref, wga_ref, wgh_ref, wa_ref, wh_ref = refs[:7]
    cvt_src = refs[7:7 + n_cvt]
    o_ref = refs[7 + n_cvt]
    cvt_dst = refs[8 + n_cvt:]
    h = h_ref[...]
    ga = _dot(h, wga_ref[...])
    gh = _dot(h, wgh_ref[...])
    ba = _dot(oa_ref[...], wa_ref[...])
    bh = _dot(oh_ref[...], wh_ref[...])
    o_ref[...] = (jax.nn.sigmoid(ga) * ba + jax.nn.sigmoid(gh) * bh).astype(o_ref.dtype)
    _run_cvts(cvt_src, cvt_dst)


def _merge(h, o_attn, o_hgrn, w_in_b, w_branch_b, *, gate_col0, tm, tn, cvts=()):
    m, d = h.shape
    tm, tn = min(tm, m), min(tn, d)
    n_tiles = d // tn
    assert ATTN_Q_DIM % HG_DIM == 0 and gate_col0 % tn == 0
    g0 = gate_col0 // tn
    grid = (m // tm, n_tiles)
    c_in, c_out, c_shapes, c_args = _cvt_plumbing(cvts, grid[0] * grid[1], lambda i, j: i * n_tiles + j)
    return pl.pallas_call(
        functools.partial(_merge_kernel, n_cvt=len(cvts)),
        grid=grid,
        in_specs=[
            pl.BlockSpec((tm, d), lambda i, j: (i, 0)),
            pl.BlockSpec((tm, ATTN_Q_DIM), lambda i, j: (i, 0)),
            pl.BlockSpec((tm, HG_DIM), lambda i, j: (i, 0)),
            pl.BlockSpec((d, tn), lambda i, j: (0, g0 + j)),
            pl.BlockSpec((d, tn), lambda i, j: (0, g0 + n_tiles + j)),
            pl.BlockSpec((ATTN_Q_DIM, tn), lambda i, j: (0, j)),
            pl.BlockSpec((HG_DIM, tn), lambda i, j: (ATTN_Q_DIM // HG_DIM, j)),
        ] + c_in,
        out_specs=[pl.BlockSpec((tm, tn), lambda i, j: (i, j))] + c_out,
        out_shape=[jax.ShapeDtypeStruct((m, d), _BF16)] + c_shapes,
        compiler_params=_compiler_params(2),
        name="gated_merge",
    )(h, o_attn, o_hgrn, w_in_b, w_in_b, w_branch_b, w_branch_b, *c_args)


def _bucket_table():
    ki = np.arange(2 * ATTN_BLOCK)[:, None]
    qi = np.arange(ATTN_BLOCK)[None, :]
    dist = qi + ATTN_BLOCK - ki
    in_window = (dist >= 0) & (dist < ATTN_BLOCK)
    in_window_first = in_window & (ki >= ATTN_BLOCK)
    dist = np.maximum(dist, 0)
    max_exact = NUM_BUCKETS // 2
    d = np.maximum(dist, 1).astype(np.float64)
    val = np.log(d / max_exact) / math.log(MAX_DISTANCE / max_exact) * (NUM_BUCKETS - max_exact)
    large = max_exact + np.trunc(val).astype(np.int64)
    frac = np.abs(val - np.round(val))[in_window & (dist > max_exact)]
    assert frac.min() > 1e-3
    bucket = np.where(dist < max_exact, dist, np.minimum(large, NUM_BUCKETS - 1))
    return bucket.reshape(-1), in_window_first.reshape(-1), in_window.reshape(-1)


def _bias_kernel(rbt_ref, onehot_ref, mask_ref, o_ref):
    a1, a2, a3 = _split3_bf16(rbt_ref[...])
    lhs = jnp.concatenate([a1, a2, a3], axis=1)
    base = _dot(lhs, onehot_ref[...])
    o_ref[0] = base + mask_ref[0]
    o_ref[1] = base + mask_ref[1]


def _bias_band(rel_bias, *, tn=4096):
    bucket, first, other = _bucket_table()
    n = bucket.shape[0]
    onehot = (np.arange(NUM_BUCKETS)[:, None] == bucket[None, :]).astype(np.float32)
    onehot3 = jnp.asarray(np.concatenate([onehot] * 3, axis=0), dtype=_BF16)
    maskadd = jnp.asarray(np.where(np.stack([first, other])[:, None, :], 0.0, -np.inf), dtype=_F32)
    out = pl.pallas_call(
        _bias_kernel,
        grid=(n // tn,),
        in_specs=[pl.BlockSpec((ATTN_HEADS, NUM_BUCKETS), lambda j: (0, 0)),
                  pl.BlockSpec((3 * NUM_BUCKETS, tn), lambda j: (0, j)),
                  pl.BlockSpec((2, 1, tn), lambda j: (0, 0, j))],
        out_specs=pl.BlockSpec((2, ATTN_HEADS, tn), lambda j: (0, 0, j)),
        out_shape=jax.ShapeDtypeStruct((2, ATTN_HEADS, n), _F32),
        compiler_params=_compiler_params(1),
        name="rel_bias_band",
    )(rel_bias.astype(_F32).T, onehot3, maskadd)
    return out.reshape(2, ATTN_HEADS, 2 * ATTN_BLOCK, ATTN_BLOCK)


def _group_rms(x, blockdiag, gain):
    x2 = x * x
    hi = x2.astype(_BF16)
    lo = (x2 - hi.astype(_F32)).astype(_BF16)
    ss = _dot(hi, blockdiag) + _dot(lo, blockdiag)
    return x * lax.rsqrt(ss * (1.0 / ATTN_HEAD_DIM) + EPS) * gain


def _pair_stack(slab, low_half):
    swapped = pltpu.roll(slab, ATTN_HEAD_DIM, 1)
    zero = jnp.zeros_like(slab)
    even = jnp.concatenate([jnp.where(low_half, slab, zero), jnp.where(low_half, zero, swapped)], axis=0)
    odd = jnp.concatenate([jnp.where(low_half, swapped, zero), jnp.where(low_half, zero, slab)], axis=0)
    return even.astype(_BF16), odd.astype(_BF16)


class _AttnSteps(NamedTuple):
    prologue: Callable[[], None]
    scores: Callable[[int], None]
    finish: Callable[[int], None]
    n_pairs: int


def _attn_steps(sink_ref, q_ref, kvc_ref, kvp_ref, bias_ref, gq_ref, gk_ref, bd_ref, o_ref):
    w = 2 * ATTN_BLOCK
    st = {}

    def prologue():
        bd = bd_ref[...]
        kv = jnp.concatenate([kvp_ref[...], kvc_ref[...]], axis=0)
        kn = _group_rms(kv[:, :ATTN_KV_DIM], bd, gk_ref[...])
        v = kv[:, ATTN_KV_DIM:]
        lane = lax.broadcasted_iota(jnp.int32, (w, V7X_LANES), 1)
        low_half = lane < ATTN_HEAD_DIM
        k_stacks, v_stacks = [], []
        for s in range(ATTN_KV_DIM // V7X_LANES):
            k_stacks.extend(_pair_stack(kn[:, s * V7X_LANES:(s + 1) * V7X_LANES], low_half))
            v_stacks.extend(_pair_stack(v[:, s * V7X_LANES:(s + 1) * V7X_LANES], low_half))
        st["k"], st["v"], st["bd"] = k_stacks, v_stacks, bd

    def kv_head(pair):
        return (pair * V7X_LANES) // (ATTN_GROUP * ATTN_HEAD_DIM)

    def scores(pair):
        s, half = divmod(pair, 2)
        if half == 0:
            q = q_ref[:, s * w:(s + 1) * w]
            st["qn"] = (_group_rms(q, st["bd"], gq_ref[...]) * ATTN_SCALE).astype(_BF16)
        qn = st["qn"][:, half * V7X_LANES:(half + 1) * V7X_LANES]
        st["sc", pair] = _nt_dot(st["k"][kv_head(pair)], qn)

    def finish(pair):
        sc = st.pop(("sc", pair))
        probs = []
        for e in range(2):
            head = 2 * pair + e
            se = sc[e * w:(e + 1) * w] + bias_ref[head]
            sink = sink_ref[head]
            mx = jnp.maximum(jnp.max(se, axis=0, keepdims=True), sink)
            p = jnp.exp(se - mx)
            denom = jnp.sum(p, axis=0, keepdims=True) + jnp.exp(sink - mx)
            probs.append((p * (1.0 / denom)).astype(_BF16))
        o = _tn_dot(jnp.concatenate(probs, axis=0), st["v"][kv_head(pair)])
        o_ref[:, pair * V7X_LANES:(pair + 1) * V7X_LANES] = o.astype(o_ref.dtype)

    return _AttnSteps(prologue, scores, finish, ATTN_HEADS // 2)


def _proj_attn_kernel(a_ref, w_ref, sink_ref, q_ref, kvc_ref, kvp_ref, bias_ref, gq_ref, gk_ref, bd_ref,
                      hg_ref, o_ref):
    attn = _attn_steps(sink_ref, q_ref, kvc_ref, kvp_ref, bias_ref, gq_ref, gk_ref, bd_ref, o_ref)
    kc = a_ref.shape[1] // attn.n_pairs
    attn.prologue()
    attn.scores(0)
    for p in range(attn.n_pairs):
        part = _dot(a_ref[:, p * kc:(p + 1) * kc], w_ref[p * kc:(p + 1) * kc, :])
        if p + 1 < attn.n_pairs:
            attn.scores(p + 1)
        if p == 0:
            hg_ref[...] = part
        else:
            hg_ref[...] += part
        attn.finish(p)


def _proj_hgrn_with_attention(h, w_in_b, col_block0, qkv, sinks, bias, q_gain, k_gain, batch, seq, *, tm, tn):
    m, k = h.shape
    tm = min(tm, m)
    nb = seq // ATTN_BLOCK
    grid = (m // tm, HGRN_IN_DIM // tn)
    nj = grid[1]
    assert grid[0] * nj == batch * nb and k % (ATTN_HEADS // 2) == 0
    kv_col = ATTN_Q_DIM // (2 * ATTN_KV_DIM)
    assert ATTN_Q_DIM % (2 * ATTN_KV_DIM) == 0
    blockdiag = jnp.asarray(np.kron(np.eye(2 * ATTN_BLOCK // ATTN_HEAD_DIM), np.ones((ATTN_HEAD_DIM,) * 2)), _BF16)
    tile = lambda g: jnp.tile(g.reshape(1, ATTN_HEAD_DIM).astype(_F32), (1, 2 * ATTN_BLOCK // ATTN_HEAD_DIM))
    blk = lambda i, j: i * nj + j
    first = lambda i, j: (blk(i, j) % nb) == 0
    return pl.pallas_call(
        _proj_attn_kernel,
        grid=grid,
        in_specs=[
            pl.BlockSpec((tm, k), lambda i, j: (i, 0)),
            pl.BlockSpec((k, tn), lambda i, j: (0, j + col_block0)),
            pl.BlockSpec(memory_space=pltpu.SMEM),
            pl.BlockSpec((ATTN_BLOCK, ATTN_Q_DIM), lambda i, j: (blk(i, j), 0)),
            pl.BlockSpec((ATTN_BLOCK, 2 * ATTN_KV_DIM), lambda i, j: (blk(i, j), kv_col)),
            pl.BlockSpec((ATTN_BLOCK, 2 * ATTN_KV_DIM),
                         lambda i, j: (jnp.where(first(i, j), blk(i, j), blk(i, j) - 1), kv_col)),
            pl.BlockSpec((None, ATTN_HEADS, 2 * ATTN_BLOCK, ATTN_BLOCK),
                         lambda i, j: (jnp.where(first(i, j), 0, 1), 0, 0, 0)),
            pl.BlockSpec((1, 2 * ATTN_BLOCK), lambda i, j: (0, 0)),
            pl.BlockSpec((1, 2 * ATTN_BLOCK), lambda i, j: (0, 0)),
            pl.BlockSpec((2 * ATTN_BLOCK, 2 * ATTN_BLOCK), lambda i, j: (0, 0)),
        ],
        out_specs=[pl.BlockSpec((tm, tn), lambda i, j: (i, j)),
                   pl.BlockSpec((ATTN_BLOCK, ATTN_Q_DIM), lambda i, j: (blk(i, j), 0))],
        out_shape=[jax.ShapeDtypeStruct((m, HGRN_IN_DIM), _F32),
                   jax.ShapeDtypeStruct((m, ATTN_Q_DIM), _BF16)],
        compiler_params=_compiler_params(2),
        name="proj_hgrn_attn",
    )(h, w_in_b, sinks.astype(_F32), qkv, qkv, qkv, bias, tile(q_gain), tile(k_gain), blockdiag)


_HG_LEVELS = tuple(HG_CHUNK >> (i + 1) for i in range(int(math.log2(HG_CHUNK))))
_HG_PIPELINE_LEAD = 2


def _hgrn_tables():
    c = HG_CHUNK
    t = np.arange(c)
    blocks = [(t[None, :] <= t[:, None]), (t[None, :] > t[:, None])]
    masks = [np.eye(c, dtype=bool)]
    for m in _HG_LEVELS:
        ref = (t // (2 * m)) * (2 * m) + m - 1
        upper = (t % (2 * m)) >= m
        up_rows = upper[:, None] & (t[None, :] > ref[:, None]) & (t[None, :] <= t[:, None])
        lo_rows = (~upper)[:, None] & (t[None, :] > t[:, None]) & (t[None, :] <= ref[:, None])
        blocks.append(up_rows | lo_rows)
        same = (t[:, None] // (2 * m)) == (t[None, :] // (2 * m))
        masks.append(same & upper[:, None] & (~upper)[None, :])
    sums = np.concatenate(blocks, axis=0).astype(np.float32)
    return np.concatenate([sums] * 3, axis=1), np.stack(masks).astype(np.float32)


def _hgrn_kernel(p_ref, lbl_ref, gain_ref, sums_ref, masks_ref, o_ref, state_ref, *, layer, rows):
    c = HG_CHUNK

    @pl.when(pl.program_id(1) == 0)
    def _():
        state_ref[...] = jnp.zeros_like(state_ref)

    lg = lbl_ref[...]
    e = jnp.exp(lg - jnp.max(lg, axis=0, keepdims=True))
    sm = e / jnp.sum(e, axis=0, keepdims=True)
    lb_all = jnp.zeros_like(sm[0:1])
    for i in range(1, layer + 1):
        lb_all = lb_all + sm[i:i + 1]

    row = lax.broadcasted_iota(jnp.int32, (c, 1), 0)
    gain = gain_ref[...]
    sums = sums_ref[...]

    def chunk_body(ci, carry):
        r0 = pl.multiple_of(ci * c, c)
        st = [dict() for _ in range(HG_HEADS)]

        def seg(h, which):
            return p_ref[pl.ds(r0, c), which * HG_DIM + h * HG_DK:which * HG_DIM + (h + 1) * HG_DK]

        def gates(h):
            lb = lb_all[:, h * HG_DK:(h + 1) * HG_DK]
            hq = seg(h, 0)
            forget = lb + (1.0 - lb) * jax.nn.sigmoid(seg(h, 1))
            d = st[h]
            d["kk"] = 1.0 - forget
            d["qf"] = hq * jax.nn.sigmoid(hq)
            d["expo"] = _dot(sums, jnp.concatenate(_split3_bf16(jnp.log(forget)), axis=0))

        def products(h):
            d = st[h]
            qf, kk = d.pop("qf"), d.pop("kk")
            d["v"] = seg(h, 2).astype(_BF16)
            decay = jnp.exp(d.pop("expo"))
            e_cum, e_end = decay[0:c], decay[c:2 * c]
            state_t = state_ref[h]
            d["o_inter"] = _nt_dot((qf * e_cum).astype(_BF16), state_t.astype(_BF16))
            parts = [_nt_dot(qf.astype(_BF16), kk.astype(_BF16))]
            for li, m in enumerate(_HG_LEVELS):
                upper = (row & m) != 0
                z = (jnp.where(upper, qf, kk) * decay[(2 + li) * c:(3 + li) * c]).astype(_BF16)
                parts.append(_nt_dot(z, z))
            d["parts"] = parts
            state_ref[h] = state_t * e_cum[c - 1:c] + _tn_dot(d["v"], (kk * e_end).astype(_BF16))

        def output(h):
            d = st[h]
            parts = d.pop("parts")
            scores = parts[0] * masks_ref[0]
            for li in range(len(_HG_LEVELS)):
                scores = scores + parts[1 + li] * masks_ref[1 + li]
            o = d.pop("o_inter") + _dot(scores.astype(_BF16), d.pop("v"))
            hg = seg(h, 3)
            y = o * lax.rsqrt(jnp.mean(o * o, axis=-1, keepdims=True) + EPS) * gain
            y = y * (hg * jax.nn.sigmoid(hg))
            o_ref[pl.ds(r0, c), h * HG_DV:(h + 1) * HG_DV] = y.astype(o_ref.dtype)

        lead = _HG_PIPELINE_LEAD
        for h in range(-2 * lead, HG_HEADS):
            if 0 <= h + 2 * lead < HG_HEADS:
                gates(h + 2 * lead)
            if 0 <= h + lead < HG_HEADS:
                products(h + lead)
            if 0 <= h:
                output(h)
        return carry

    lax.fori_loop(0, rows // c, chunk_body, 0)


def _hgrn(p, lb_logits, norm_gain, layer, batch, seq, *, rows=256):
    rows = min(rows, seq)
    steps = seq // rows
    depth = lb_logits.shape[0]
    sums, masks = _hgrn_tables()
    return pl.pallas_call(
        functools.partial(_hgrn_kernel, layer=layer, rows=rows),
        grid=(batch, steps),
        in_specs=[
            pl.BlockSpec((rows, HGRN_IN_DIM), lambda b, s: (b * steps + s, 0)),
            pl.BlockSpec((depth, HG_DIM), lambda b, s: (0, 0)),
            pl.BlockSpec((1, HG_DV), lambda b, s: (0, 0)),
            pl.BlockSpec(sums.shape, lambda b, s: (0, 0)),
            pl.BlockSpec(masks.shape, lambda b, s: (0, 0, 0)),
        ],
        out_specs=pl.BlockSpec((rows, HG_DIM), lambda b, s: (b * steps + s, 0)),
        out_shape=jax.ShapeDtypeStruct((batch * seq, HG_DIM), _BF16),
        scratch_shapes=[pltpu.VMEM((HG_HEADS, HG_DV, HG_DK), _F32)],
        compiler_params=_compiler_params(2),
        name="hgrn2_scan",
    )(p, lb_logits.astype(_F32), norm_gain.reshape(1, HG_DV).astype(_F32),
      jnp.asarray(sums, _BF16), jnp.asarray(masks, _F32))


def kernel(x, attn_norm_gain, w_in, q_norm_gain, k_norm_gain, attn_sinks, rel_bias,
           hgrn_lb_logits, hgrn_norm_gain, w_branch, w_out, mlp_norm_gain, w_up, w_down):
    batch, seq, d = x.shape
    depth = w_in.shape[0]
    m = batch * seq
    xf = x.reshape(m, d).astype(_F32)
    bias = _bias_band(rel_bias)
    t = _TILES
    hg_col0 = QKV_DIM // t.proj_tn
    gate_col0 = QKV_DIM + HGRN_IN_DIM
    assert QKV_DIM % t.proj_tn == 0

    w_in_b = w_in[0].astype(_BF16)
    w_br_b = w_branch[0].astype(_BF16)
    w_o_b = w_out[0].astype(_BF16)
    for l in range(depth):
        h = _rmsnorm(xf, attn_norm_gain[l])
        qkv, = _matmul(h, w_in_b, n_out=QKV_DIM, tm=t.tm, tn=t.proj_tn, out_dtype=_F32, name="proj_qkv")
        hgp, o_attn = _proj_hgrn_with_attention(h, w_in_b, hg_col0, qkv, attn_sinks[l], bias, q_norm_gain[l],
                                                k_norm_gain[l], batch, seq, tm=t.tm, tn=t.proj_tn)
        o_hgrn = _hgrn(hgp, hgrn_lb_logits, hgrn_norm_gain[l], l, batch, seq)
        merged, w_u_b = _merge(h, o_attn, o_hgrn, w_in_b, w_br_b, gate_col0=gate_col0,
                               tm=t.tm, tn=t.merge_tn, cvts=(_Cvt(w_up, l),))
        xf, = _matmul(merged, w_o_b, n_out=d, tm=t.tm, tn=t.proj_tn, out_dtype=_F32, residual=xf,
                      name="out_proj")

        h2 = _rmsnorm(xf, mlp_norm_gain[l])
        u, w_d_b = _matmul(h2, w_u_b, n_out=w_up.shape[2], tm=t.tm, tn=t.up_tn, out_dtype=_BF16,
                           epilogue=_relu2, cvts=(_Cvt(w_down, l),), name="mlp_up")
        nxt = () if l + 1 == depth else (_Cvt(w_in, l + 1), _Cvt(w_branch, l + 1), _Cvt(w_out, l + 1))
        xf, *nxt_b = _matmul_ktiled_res(u, w_d_b, xf, tm=t.tm, tn=t.down_tn, tk=t.down_tk, cvts=nxt,
                                        name="mlp_down")
        if nxt_b:
            w_in_b, w_br_b, w_o_b = nxt_b
    return xf.reshape(batch, seq, d).astype(x.dtype)
```

```python
import functools
import math
from typing import Callable, NamedTuple

import numpy as np
import jax
import jax.numpy as jnp
from jax import lax
from jax.experimental import pallas as pl
from jax.experimental.pallas import tpu as pltpu

ATTN_HEADS = 32
ATTN_KV_HEADS = 4
ATTN_HEAD_DIM = 64
ATTN_GROUP = ATTN_HEADS // ATTN_KV_HEADS
ATTN_BLOCK = 128
ATTN_SCALE = ATTN_HEAD_DIM ** -0.5
NUM_BUCKETS = 32
MAX_DISTANCE = 128
HG_HEADS = 8
HG_DK = 128
HG_DV = 128
HG_CHUNK = 64
EPS = 1e-6

ATTN_Q_DIM = ATTN_HEADS * ATTN_HEAD_DIM
ATTN_KV_DIM = ATTN_KV_HEADS * ATTN_HEAD_DIM
QKV_DIM = ATTN_Q_DIM + 2 * ATTN_KV_DIM
HG_DIM = HG_HEADS * HG_DK
HGRN_IN_DIM = 4 * HG_DIM

V7X_LANES = 128
BF16_SUBLANES = 16
V7X_VMEM_LIMIT_BYTES = 56 * 1024 * 1024

_BF16 = jnp.bfloat16
_F32 = jnp.float32


class _Tiles(NamedTuple):
    tm: int = 1024
    proj_tn: int = 512
    merge_tn: int = 256
    up_tn: int = 1024
    down_tn: int = 1024
    down_tk: int = 2048


_TILES = _Tiles()


def _nt_dot(a, b):
    return lax.dot_general(a, b, (((1,), (1,)), ((), ())), preferred_element_type=_F32)


def _tn_dot(a, b):
    return lax.dot_general(a, b, (((0,), (0,)), ((), ())), preferred_element_type=_F32)


def _dot(a, b):
    return jnp.dot(a, b, preferred_element_type=_F32)


def _split3_bf16(x):
    a1 = x.astype(_BF16)
    r1 = x - a1.astype(_F32)
    a2 = r1.astype(_BF16)
    a3 = (r1 - a2.astype(_F32)).astype(_BF16)
    return a1, a2, a3


def _compiler_params(n_grid):
    return pltpu.CompilerParams(
        dimension_semantics=("arbitrary",) * n_grid,
        vmem_limit_bytes=V7X_VMEM_LIMIT_BYTES,
    )


class _Normed(NamedTuple):
    xg: jax.Array
    ss: jax.Array


def _emit_normed(y, first_col_tile, gain_ref, xg_ref, ss_ref):
    xg_ref[...] = (y * gain_ref[...]).astype(xg_ref.dtype)
    part = jnp.broadcast_to(jnp.sum(y * y, axis=1, keepdims=True), ss_ref.shape)

    @pl.when(first_col_tile)
    def _():
        ss_ref[...] = part

    @pl.when(jnp.logical_not(first_col_tile))
    def _():
        ss_ref[...] += part


def _scale_rows_rms(y, ss_ref, d):
    r = lax.rsqrt(ss_ref[...] * (1.0 / d) + EPS)
    return jnp.concatenate([y[:, c * V7X_LANES:(c + 1) * V7X_LANES] * r for c in range(y.shape[1] // V7X_LANES)],
                           axis=1)


def _normed_input_kernel(x_ref, g_ref, xg_ref, ss_ref):
    _emit_normed(x_ref[...], True, g_ref, xg_ref, ss_ref)


def _normed_input(x, gain, *, rows=256):
    m, d = x.shape
    rows = min(rows, m)
    xg, ss = pl.pallas_call(
        _normed_input_kernel,
        grid=(m // rows,),
        in_specs=[pl.BlockSpec((rows, d), lambda i: (i, 0)),
                  pl.BlockSpec((1, d), lambda i: (0, 0))],
        out_specs=[pl.BlockSpec((rows, d), lambda i: (i, 0)),
                   pl.BlockSpec((rows, V7X_LANES), lambda i: (i, 0))],
        out_shape=[jax.ShapeDtypeStruct((m, d), _BF16), jax.ShapeDtypeStruct((m, V7X_LANES), _F32)],
        compiler_params=_compiler_params(1),
        name="normed_input",
    )(x, gain.reshape(1, d).astype(_F32))
    return _Normed(xg, ss)


class _Cvt(NamedTuple):
    stacked: jax.Array
    layer: int


def _cvt_plumbing(cvts, n_steps, linear_step):
    in_specs, out_specs, out_shapes, args = [], [], [], []
    for c in cvts:
        _, rows, cols = c.stacked.shape
        units = rows // BF16_SUBLANES
        assert rows % BF16_SUBLANES == 0
        blocks = max(b for b in range(1, min(units, n_steps) + 1) if units % b == 0)
        brows = rows // blocks
        idx = functools.partial(lambda *g, blocks: (linear_step(*g) * blocks) // n_steps, blocks=blocks)
        in_specs.append(pl.BlockSpec((None, brows, cols),
                                     functools.partial(lambda *g, idx, layer: (layer, idx(*g), 0), idx=idx, layer=c.layer)))
        out_specs.append(pl.BlockSpec((brows, cols), functools.partial(lambda *g, idx: (idx(*g), 0), idx=idx)))
        out_shapes.append(jax.ShapeDtypeStruct((rows, cols), _BF16))
        args.append(c.stacked)
    return in_specs, out_specs, out_shapes, args


def _run_cvts(src_refs, dst_refs):
    for s, d in zip(src_refs, dst_refs):
        d[...] = s[...].astype(d.dtype)


def _relu2(y):
    r = jnp.maximum(y, 0.0)
    return r * r


def _mm_kernel(*refs, epilogue, has_row_ss, has_residual, has_norm, n_cvt):
    refs = list(refs)
    a_ref, w_ref = refs.pop(0), refs.pop(0)
    ss_in_ref = refs.pop(0) if has_row_ss else None
    r_ref = refs.pop(0) if has_residual else None
    gain_ref = refs.pop(0) if has_norm else None
    cvt_src = [refs.pop(0) for _ in range(n_cvt)]
    o_ref = refs.pop(0)
    xg_ref, ss_ref = (refs.pop(0), refs.pop(0)) if has_norm else (None, None)
    cvt_dst = refs
    y = _dot(a_ref[...], w_ref[...])
    if has_row_ss:
        y = _scale_rows_rms(y, ss_in_ref, a_ref.shape[1])
    if epilogue is not None:
        y = epilogue(y)
    if has_residual:
        y = r_ref[...] + y
    o_ref[...] = y.astype(o_ref.dtype)
    _run_cvts(cvt_src, cvt_dst)
    if has_norm:
        _emit_normed(y, pl.program_id(1) == 0, gain_ref, xg_ref, ss_ref)


def _matmul(a, w, *, n_out, col_block0=0, tm, tn, out_dtype, row_ss=None, epilogue=None, residual=None,
            norm_gain=None, cvts=(), name):
    m, k = a.shape
    tm, tn = min(tm, m), min(tn, n_out)
    grid = (m // tm, n_out // tn)
    n_steps = grid[0] * grid[1]
    in_specs = [pl.BlockSpec((tm, k), lambda i, j: (i, 0)),
                pl.BlockSpec((k, tn), lambda i, j: (0, j + col_block0))]
    args = [a, w]
    out_specs = [pl.BlockSpec((tm, tn), lambda i, j: (i, j))]
    out_shapes = [jax.ShapeDtypeStruct((m, n_out), out_dtype)]
    if row_ss is not None:
        in_specs.append(pl.BlockSpec((tm, V7X_LANES), lambda i, j: (i, 0)))
        args.append(row_ss)
    if residual is not None:
        in_specs.append(pl.BlockSpec((tm, tn), lambda i, j: (i, j)))
        args.append(residual)
    if norm_gain is not None:
        in_specs.append(pl.BlockSpec((1, tn), lambda i, j: (0, j)))
        args.append(norm_gain.reshape(1, n_out).astype(_F32))
        out_specs += [pl.BlockSpec((tm, tn), lambda i, j: (i, j)),
                      pl.BlockSpec((tm, V7X_LANES), lambda i, j: (i, 0))]
        out_shapes += [jax.ShapeDtypeStruct((m, n_out), _BF16), jax.ShapeDtypeStruct((m, V7X_LANES), _F32)]
    c_in, c_out, c_shapes, c_args = _cvt_plumbing(cvts, n_steps, lambda i, j: i * grid[1] + j)
    return pl.pallas_call(
        functools.partial(_mm_kernel, epilogue=epilogue, has_row_ss=row_ss is not None,
                          has_residual=residual is not None, has_norm=norm_gain is not None, n_cvt=len(cvts)),
        grid=grid,
        in_specs=in_specs + c_in,
        out_specs=out_specs + c_out,
        out_shape=out_shapes + c_shapes,
        compiler_params=_compiler_params(2),
        name=name,
    )(*args, *c_args)


def _mm_ktiled_res_kernel(*refs, has_norm, n_cvt):
    refs = list(refs)
    a_ref, w_ref, r_ref = refs.pop(0), refs.pop(0), refs.pop(0)
    gain_ref = refs.pop(0) if has_norm else None
    cvt_src = [refs.pop(0) for _ in range(n_cvt)]
    o_ref = refs.pop(0)
    xg_ref, ss_ref = (refs.pop(0), refs.pop(0)) if has_norm else (None, None)
    cvt_dst = refs

    @pl.when(pl.program_id(2) == 0)
    def _():
        o_ref[...] = r_ref[...]

    o_ref[...] += _dot(a_ref[...], w_ref[...])
    _run_cvts(cvt_src, cvt_dst)

    if has_norm:
        @pl.when(pl.program_id(2) == pl.num_programs(2) - 1)
        def _():
            _emit_normed(o_ref[...], pl.program_id(1) == 0, gain_ref, xg_ref, ss_ref)


def _matmul_ktiled_res(a, w, residual, *, tm, tn, tk, norm_gain=None, cvts=(), name):
    m, k = a.shape
    _, n = w.shape
    tm, tn, tk = min(tm, m), min(tn, n), min(tk, k)
    grid = (m // tm, n // tn, k // tk)
    n_steps = grid[0] * grid[1] * grid[2]
    in_specs = [pl.BlockSpec((tm, tk), lambda i, j, q: (i, q)),
                pl.BlockSpec((tk, tn), lambda i, j, q: (q, j)),
                pl.BlockSpec((tm, tn), lambda i, j, q: (i, j))]
    args = [a, w, residual]
    out_specs = [pl.BlockSpec((tm, tn), lambda i, j, q: (i, j))]
    out_shapes = [jax.ShapeDtypeStruct((m, n), _F32)]
    if norm_gain is not None:
        in_specs.append(pl.BlockSpec((1, tn), lambda i, j, q: (0, j)))
        args.append(norm_gain.reshape(1, n).astype(_F32))
        out_specs += [pl.BlockSpec((tm, tn), lambda i, j, q: (i, j)),
                      pl.BlockSpec((tm, V7X_LANES), lambda i, j, q: (i, 0))]
        out_shapes += [jax.ShapeDtypeStruct((m, n), _BF16), jax.ShapeDtypeStruct((m, V7X_LANES), _F32)]
    c_in, c_out, c_shapes, c_args = _cvt_plumbing(
        cvts, n_steps, lambda i, j, q: (i * grid[1] + j) * grid[2] + q)
    return pl.pallas_call(
        functools.partial(_mm_ktiled_res_kernel, has_norm=norm_gain is not None, n_cvt=len(cvts)),
        grid=grid,
        in_specs=in_specs + c_in,
        out_specs=out_specs + c_out,
        out_shape=out_shapes + c_shapes,
        compiler_params=_compiler_params(3),
        name=name,
    )(*args, *c_args)


def _merge_kernel(*refs, n_cvt):
    h_ref, ss_ref, oa_ref, oh_ref, wga_ref, wgh_ref, wa_ref, wh_ref = refs[:8]
    cvt_src = refs[8:8 + n_cvt]
    o_ref = refs[8 + n_cvt]
    cvt_dst = refs[9 + n_cvt:]
    h = h_ref[...]
    d = h_ref.shape[1]
    ga = _scale_rows_rms(_dot(h, wga_ref[...]), ss_ref, d)
    gh = _scale_rows_rms(_dot(h, wgh_ref[...]), ss_ref, d)
    ba = _dot(oa_ref[...], wa_ref[...])
    bh = _dot(oh_ref[...], wh_ref[...])
    o_ref[...] = (jax.nn.sigmoid(ga) * ba + jax.nn.sigmoid(gh) * bh).astype(o_ref.dtype)
    _run_cvts(cvt_src, cvt_dst)


def _merge(nx, o_attn, o_hgrn, w_in_b, w_branch_b, *, gate_col0, tm, tn, cvts=()):
    m, d = nx.xg.shape
    tm, tn = min(tm, m), min(tn, d)
    n_tiles = d // tn
    assert ATTN_Q_DIM % HG_DIM == 0 and gate_col0 % tn == 0
    g0 = gate_col0 // tn
    grid = (m // tm, n_tiles)
    c_in, c_out, c_shapes, c_args = _cvt_plumbing(cvts, grid[0] * grid[1], lambda i, j: i * n_tiles + j)
    return pl.pallas_call(
        functools.partial(_merge_kernel, n_cvt=len(cvts)),
        grid=grid,
        in_specs=[
            pl.BlockSpec((tm, d), lambda i, j: (i, 0)),
            pl.BlockSpec((tm, V7X_LANES), lambda i, j: (i, 0)),
            pl.BlockSpec((tm, ATTN_Q_DIM), lambda i, j: (i, 0)),
            pl.BlockSpec((tm, HG_DIM), lambda i, j: (i, 0)),
            pl.BlockSpec((d, tn), lambda i, j: (0, g0 + j)),
            pl.BlockSpec((d, tn), lambda i, j: (0, g0 + n_tiles + j)),
            pl.BlockSpec((ATTN_Q_DIM, tn), lambda i, j: (0, j)),
            pl.BlockSpec((HG_DIM, tn), lambda i, j: (ATTN_Q_DIM // HG_DIM, j)),
        ] + c_in,
        out_specs=[pl.BlockSpec((tm, tn), lambda i, j: (i, j))] + c_out,
        out_shape=[jax.ShapeDtypeStruct((m, d), _BF16)] + c_shapes,
        compiler_params=_compiler_params(2),
        name="gated_merge",
    )(nx.xg, nx.ss, o_attn, o_hgrn, w_in_b, w_in_b, w_branch_b, w_branch_b, *c_args)


def _bucket_table():
    ki = np.arange(2 * ATTN_BLOCK)[:, None]
    qi = np.arange(ATTN_BLOCK)[None, :]
    dist = qi + ATTN_BLOCK - ki
    in_window = (dist >= 0) & (dist < ATTN_BLOCK)
    in_window_first = in_window & (ki >= ATTN_BLOCK)
    dist = np.maximum(dist, 0)
    max_exact = NUM_BUCKETS // 2
    d = np.maximum(dist, 1).astype(np.float64)
    val = np.log(d / max_exact) / math.log(MAX_DISTANCE / max_exact) * (NUM_BUCKETS - max_exact)
    large = max_exact + np.trunc(val).astype(np.int64)
    frac = np.abs(val - np.round(val))[in_window & (dist > max_exact)]
    assert frac.min() > 1e-3
    bucket = np.where(dist < max_exact, dist, np.minimum(large, NUM_BUCKETS - 1))
    return bucket.reshape(-1), in_window_first.reshape(-1), in_window.reshape(-1)


def _bias_kernel(rbt_ref, onehot_ref, mask_ref, o_ref):
    a1, a2, a3 = _split3_bf16(rbt_ref[...])
    lhs = jnp.concatenate([a1, a2, a3], axis=1)
    base = _dot(lhs, onehot_ref[...])
    o_ref[0] = base + mask_ref[0]
    o_ref[1] = base + mask_ref[1]


def _bias_band(rel_bias, *, tn=4096):
    bucket, first, other = _bucket_table()
    n = bucket.shape[0]
    onehot = (np.arange(NUM_BUCKETS)[:, None] == bucket[None, :]).astype(np.float32)
    onehot3 = jnp.asarray(np.concatenate([onehot] * 3, axis=0), dtype=_BF16)
    maskadd = jnp.asarray(np.where(np.stack([first, other])[:, None, :], 0.0, -np.inf), dtype=_F32)
    out = pl.pallas_call(
        _bias_kernel,
        grid=(n // tn,),
        in_specs=[pl.BlockSpec((ATTN_HEADS, NUM_BUCKETS), lambda j: (0, 0)),
                  pl.BlockSpec((3 * NUM_BUCKETS, tn), lambda j: (0, j)),
                  pl.BlockSpec((2, 1, tn), lambda j: (0, 0, j))],
        out_specs=pl.BlockSpec((2, ATTN_HEADS, tn), lambda j: (0, 0, j)),
        out_shape=jax.ShapeDtypeStruct((2, ATTN_HEADS, n), _F32),
        compiler_params=_compiler_params(1),
        name="rel_bias_band",
    )(rel_bias.astype(_F32).T, onehot3, maskadd)
    return out.reshape(2, ATTN_HEADS, 2 * ATTN_BLOCK, ATTN_BLOCK)


def _group_rms(x, blockdiag, gain):
    x2 = x * x
    hi = x2.astype(_BF16)
    lo = (x2 - hi.astype(_F32)).astype(_BF16)
    ss = _dot(hi, blockdiag) + _dot(lo, blockdiag)
    return x * lax.rsqrt(ss * (1.0 / ATTN_HEAD_DIM) + EPS) * gain


def _pair_stack(slab, low_half):
    swapped = pltpu.roll(slab, ATTN_HEAD_DIM, 1)
    zero = jnp.zeros_like(slab)
    even = jnp.concatenate([jnp.where(low_half, slab, zero), jnp.where(low_half, zero, swapped)], axis=0)
    odd = jnp.concatenate([jnp.where(low_half, swapped, zero), jnp.where(low_half, zero, slab)], axis=0)
    return even.astype(_BF16), odd.astype(_BF16)


class _AttnSteps(NamedTuple):
    prologue: Callable[[], None]
    scores: Callable[[int], None]
    finish: Callable[[int], None]
    n_pairs: int


def _attn_steps(sink_ref, q_ref, kvc_ref, kvp_ref, bias_ref, gq_ref, gk_ref, bd_ref, o_ref):
    w = 2 * ATTN_BLOCK
    st = {}

    def prologue():
        bd = bd_ref[...]
        kv = jnp.concatenate([kvp_ref[...], kvc_ref[...]], axis=0)
        kn = _group_rms(kv[:, :ATTN_KV_DIM], bd, gk_ref[...])
        v = kv[:, ATTN_KV_DIM:]
        lane = lax.broadcasted_iota(jnp.int32, (w, V7X_LANES), 1)
        low_half = lane < ATTN_HEAD_DIM
        k_stacks, v_stacks = [], []
        for s in range(ATTN_KV_DIM // V7X_LANES):
            k_stacks.extend(_pair_stack(kn[:, s * V7X_LANES:(s + 1) * V7X_LANES], low_half))
            v_stacks.extend(_pair_stack(v[:, s * V7X_LANES:(s + 1) * V7X_LANES], low_half))
        st["k"], st["v"], st["bd"] = k_stacks, v_stacks, bd

    def kv_head(pair):
        return (pair * V7X_LANES) // (ATTN_GROUP * ATTN_HEAD_DIM)

    def scores(pair):
        s, half = divmod(pair, 2)
        if half == 0:
            q = q_ref[:, s * w:(s + 1) * w]
            st["qn"] = (_group_rms(q, st["bd"], gq_ref[...]) * ATTN_SCALE).astype(_BF16)
        qn = st["qn"][:, half * V7X_LANES:(half + 1) * V7X_LANES]
        st["sc", pair] = _nt_dot(st["k"][kv_head(pair)], qn)

    def finish(pair):
        sc = st.pop(("sc", pair))
        probs = []
        for e in range(2):
            head = 2 * pair + e
            se = sc[e * w:(e + 1) * w] + bias_ref[head]
            sink = sink_ref[head]
            mx = jnp.maximum(jnp.max(se, axis=0, keepdims=True), sink)
            p = jnp.exp(se - mx)
            denom = jnp.sum(p, axis=0, keepdims=True) + jnp.exp(sink - mx)
            probs.append((p * (1.0 / denom)).astype(_BF16))
        o = _tn_dot(jnp.concatenate(probs, axis=0), st["v"][kv_head(pair)])
        o_ref[:, pair * V7X_LANES:(pair + 1) * V7X_LANES] = o.astype(o_ref.dtype)

    return _AttnSteps(prologue, scores, finish, ATTN_HEADS // 2)


def _proj_attn_kernel(a_ref, w_ref, ss_ref, sink_ref, q_ref, kvc_ref, kvp_ref, bias_ref, gq_ref, gk_ref, bd_ref,
                      hg_ref, o_ref):
    attn = _attn_steps(sink_ref, q_ref, kvc_ref, kvp_ref, bias_ref, gq_ref, gk_ref, bd_ref, o_ref)
    kc = a_ref.shape[1] // attn.n_pairs
    attn.prologue()
    attn.scores(0)
    for p in range(attn.n_pairs):
        part = _dot(a_ref[:, p * kc:(p + 1) * kc], w_ref[p * kc:(p + 1) * kc, :])
        if p + 1 < attn.n_pairs:
            attn.scores(p + 1)
        if p == 0:
            hg_ref[...] = part
        elif p + 1 < attn.n_pairs:
            hg_ref[...] += part
        else:
            hg_ref[...] = _scale_rows_rms(hg_ref[...] + part, ss_ref, a_ref.shape[1])
        attn.finish(p)


def _proj_hgrn_with_attention(nx, w_in_b, col_block0, qkv, sinks, bias, q_gain, k_gain, batch, seq, *, tm, tn):
    m, k = nx.xg.shape
    tm = min(tm, m)
    nb = seq // ATTN_BLOCK
    grid = (m // tm, HGRN_IN_DIM // tn)
    nj = grid[1]
    assert grid[0] * nj == batch * nb and k % (ATTN_HEADS // 2) == 0
    kv_col = ATTN_Q_DIM // (2 * ATTN_KV_DIM)
    assert ATTN_Q_DIM % (2 * ATTN_KV_DIM) == 0
    blockdiag = jnp.asarray(np.kron(np.eye(2 * ATTN_BLOCK // ATTN_HEAD_DIM), np.ones((ATTN_HEAD_DIM,) * 2)), _BF16)
    tile = lambda g: jnp.tile(g.reshape(1, ATTN_HEAD_DIM).astype(_F32), (1, 2 * ATTN_BLOCK // ATTN_HEAD_DIM))
    blk = lambda i, j: i * nj + j
    first = lambda i, j: (blk(i, j) % nb) == 0
    return pl.pallas_call(
        _proj_attn_kernel,
        grid=grid,
        in_specs=[
            pl.BlockSpec((tm, k), lambda i, j: (i, 0)),
            pl.BlockSpec((k, tn), lambda i, j: (0, j + col_block0)),
            pl.BlockSpec((tm, V7X_LANES), lambda i, j: (i, 0)),
            pl.BlockSpec(memory_space=pltpu.SMEM),
            pl.BlockSpec((ATTN_BLOCK, ATTN_Q_DIM), lambda i, j: (blk(i, j), 0)),
            pl.BlockSpec((ATTN_BLOCK, 2 * ATTN_KV_DIM), lambda i, j: (blk(i, j), kv_col)),
            pl.BlockSpec((ATTN_BLOCK, 2 * ATTN_KV_DIM),
                         lambda i, j: (jnp.where(first(i, j), blk(i, j), blk(i, j) - 1), kv_col)),
            pl.BlockSpec((None, ATTN_HEADS, 2 * ATTN_BLOCK, ATTN_BLOCK),
                         lambda i, j: (jnp.where(first(i, j), 0, 1), 0, 0, 0)),
            pl.BlockSpec((1, 2 * ATTN_BLOCK), lambda i, j: (0, 0)),
            pl.BlockSpec((1, 2 * ATTN_BLOCK), lambda i, j: (0, 0)),
            pl.BlockSpec((2 * ATTN_BLOCK, 2 * ATTN_BLOCK), lambda i, j: (0, 0)),
        ],
        out_specs=[pl.BlockSpec((tm, tn), lambda i, j: (i, j)),
                   pl.BlockSpec((ATTN_BLOCK, ATTN_Q_DIM), lambda i, j: (blk(i, j), 0))],
        out_shape=[jax.ShapeDtypeStruct((m, HGRN_IN_DIM), _F32),
                   jax.ShapeDtypeStruct((m, ATTN_Q_DIM), _BF16)],
        compiler_params=_compiler_params(2),
        name="proj_hgrn_attn",
    )(nx.xg, w_in_b, nx.ss, sinks.astype(_F32), qkv, qkv, qkv, bias, tile(q_gain), tile(k_gain), blockdiag)


_HG_LEVELS = tuple(HG_CHUNK >> (i + 1) for i in range(int(math.log2(HG_CHUNK))))
_HG_PIPELINE_LEAD = 2


def _hgrn_tables():
    c = HG_CHUNK
    t = np.arange(c)
    blocks = [(t[None, :] <= t[:, None]), (t[None, :] > t[:, None])]
    masks = [np.eye(c, dtype=bool)]
    for m in _HG_LEVELS:
        ref = (t // (2 * m)) * (2 * m) + m - 1
        upper = (t % (2 * m)) >= m
        up_rows = upper[:, None] & (t[None, :] > ref[:, None]) & (t[None, :] <= t[:, None])
        lo_rows = (~upper)[:, None] & (t[None, :] > t[:, None]) & (t[None, :] <= ref[:, None])
        blocks.append(up_rows | lo_rows)
        same = (t[:, None] // (2 * m)) == (t[None, :] // (2 * m))
        masks.append(same & upper[:, None] & (~upper)[None, :])
    sums = np.concatenate(blocks, axis=0).astype(np.float32)
    return np.concatenate([sums] * 3, axis=1), np.stack(masks).astype(np.float32)


def _hgrn_kernel(p_ref, lbl_ref, gain_ref, sums_ref, masks_ref, o_ref, state_ref, *, layer, rows):
    c = HG_CHUNK

    @pl.when(pl.program_id(1) == 0)
    def _():
        state_ref[...] = jnp.zeros_like(state_ref)

    lg = lbl_ref[...]
    e = jnp.exp(lg - jnp.max(lg, axis=0, keepdims=True))
    sm = e / jnp.sum(e, axis=0, keepdims=True)
    lb_all = jnp.zeros_like(sm[0:1])
    for i in range(1, layer + 1):
        lb_all = lb_all + sm[i:i + 1]

    row = lax.broadcasted_iota(jnp.int32, (c, 1), 0)
    gain = gain_ref[...]
    sums = sums_ref[...]

    def chunk_body(ci, carry):
        r0 = pl.multiple_of(ci * c, c)
        st = [dict() for _ in range(HG_HEADS)]

        def seg(h, which):
            return p_ref[pl.ds(r0, c), which * HG_DIM + h * HG_DK:which * HG_DIM + (h + 1) * HG_DK]

        def gates(h):
            lb = lb_all[:, h * HG_DK:(h + 1) * HG_DK]
            hq = seg(h, 0)
            forget = lb + (1.0 - lb) * jax.nn.sigmoid(seg(h, 1))
            d = st[h]
            d["kk"] = 1.0 - forget
            d["qf"] = hq * jax.nn.sigmoid(hq)
            d["expo"] = _dot(sums, jnp.concatenate(_split3_bf16(jnp.log(forget)), axis=0))

        def products(h):
            d = st[h]
            qf, kk = d.pop("qf"), d.pop("kk")
            d["v"] = seg(h, 2).astype(_BF16)
            decay = jnp.exp(d.pop("expo"))
            e_cum, e_end = decay[0:c], decay[c:2 * c]
            state_t = state_ref[h]
            d["o_inter"] = _nt_dot((qf * e_cum).astype(_BF16), state_t.astype(_BF16))
            parts = [_nt_dot(qf.astype(_BF16), kk.astype(_BF16))]
            for li, m in enumerate(_HG_LEVELS):
                upper = (row & m) != 0
                z = (jnp.where(upper, qf, kk) * decay[(2 + li) * c:(3 + li) * c]).astype(_BF16)
                parts.append(_nt_dot(z, z))
            d["parts"] = parts
            state_ref[h] = state_t * e_cum[c - 1:c] + _tn_dot(d["v"], (kk * e_end).astype(_BF16))

        def output(h):
            d = st[h]
            parts = d.pop("parts")
            scores = parts[0] * masks_ref[0]
            for li in range(len(_HG_LEVELS)):
                scores = scores + parts[1 + li] * masks_ref[1 + li]
            o = d.pop("o_inter") + _dot(scores.astype(_BF16), d.pop("v"))
            hg = seg(h, 3)
            y = o * lax.rsqrt(jnp.mean(o * o, axis=-1, keepdims=True) + EPS) * gain
            y = y * (hg * jax.nn.sigmoid(hg))
            o_ref[pl.ds(r0, c), h * HG_DV:(h + 1) * HG_DV] = y.astype(o_ref.dtype)

        lead = _HG_PIPELINE_LEAD
        for h in range(-2 * lead, HG_HEADS):
            if 0 <= h + 2 * lead < HG_HEADS:
                gates(h + 2 * lead)
            if 0 <= h + lead < HG_HEADS:
                products(h + lead)
            if 0 <= h:
                output(h)
        return carry

    lax.fori_loop(0, rows // c, chunk_body, 0)


def _hgrn(p, lb_logits, norm_gain, layer, batch, seq, *, rows=256):
    rows = min(rows, seq)
    steps = seq // rows
    depth = lb_logits.shape[0]
    sums, masks = _hgrn_tables()
    return pl.pallas_call(
        functools.partial(_hgrn_kernel, layer=layer, rows=rows),
        grid=(batch, steps),
        in_specs=[
            pl.BlockSpec((rows, HGRN_IN_DIM), lambda b, s: (b * steps + s, 0)),
            pl.BlockSpec((depth, HG_DIM), lambda b, s: (0, 0)),
            pl.BlockSpec((1, HG_DV), lambda b, s: (0, 0)),
            pl.BlockSpec(sums.shape, lambda b, s: (0, 0)),
            pl.BlockSpec(masks.shape, lambda b, s: (0, 0, 0)),
        ],
        out_specs=pl.BlockSpec((rows, HG_DIM), lambda b, s: (b * steps + s, 0)),
        out_shape=jax.ShapeDtypeStruct((batch * seq, HG_DIM), _BF16),
        scratch_shapes=[pltpu.VMEM((HG_HEADS, HG_DV, HG_DK), _F32)],
        compiler_params=_compiler_params(2),
        name="hgrn2_scan",
    )(p, lb_logits.astype(_F32), norm_gain.reshape(1, HG_DV).astype(_F32),
      jnp.asarray(sums, _BF16), jnp.asarray(masks, _F32))


def kernel(x, attn_norm_gain, w_in, q_norm_gain, k_norm_gain, attn_sinks, rel_bias,
           hgrn_lb_logits, hgrn_norm_gain, w_branch, w_out, mlp_norm_gain, w_up, w_down):
    batch, seq, d = x.shape
    depth = w_in.shape[0]
    m = batch * seq
    xf = x.reshape(m, d).astype(_F32)
    bias = _bias_band(rel_bias)
    t = _TILES
    hg_col0 = QKV_DIM // t.proj_tn
    gate_col0 = QKV_DIM + HGRN_IN_DIM
    assert QKV_DIM % t.proj_tn == 0

    w_in_b = w_in[0].astype(_BF16)
    w_br_b = w_branch[0].astype(_BF16)
    w_o_b = w_out[0].astype(_BF16)
    nx = _normed_input(xf, attn_norm_gain[0])
    for l in range(depth):
        qkv, = _matmul(nx.xg, w_in_b, n_out=QKV_DIM, tm=t.tm, tn=t.proj_tn, out_dtype=_F32, row_ss=nx.ss,
                       name="proj_qkv")
        hgp, o_attn = _proj_hgrn_with_attention(nx, w_in_b, hg_col0, qkv, attn_sinks[l], bias, q_norm_gain[l],
                                                k_norm_gain[l], batch, seq, tm=t.tm, tn=t.proj_tn)
        o_hgrn = _hgrn(hgp, hgrn_lb_logits, hgrn_norm_gain[l], l, batch, seq)
        merged, w_u_b = _merge(nx, o_attn, o_hgrn, w_in_b, w_br_b, gate_col0=gate_col0,
                               tm=t.tm, tn=t.merge_tn, cvts=(_Cvt(w_up, l),))
        xf, xg, ss = _matmul(merged, w_o_b, n_out=d, tm=t.tm, tn=t.proj_tn, out_dtype=_F32, residual=xf,
                             norm_gain=mlp_norm_gain[l], name="out_proj")

        u, w_d_b = _matmul(xg, w_u_b, n_out=w_up.shape[2], tm=t.tm, tn=t.up_tn, out_dtype=_BF16,
                           row_ss=ss, epilogue=_relu2, cvts=(_Cvt(w_down, l),), name="mlp_up")
        if l + 1 == depth:
            xf, = _matmul_ktiled_res(u, w_d_b, xf, tm=t.tm, tn=t.down_tn, tk=t.down_tk, name="mlp_down")
        else:
            xf, xg, ss, w_in_b, w_br_b, w_o_b = _matmul_ktiled_res(
                u, w_d_b, xf, tm=t.tm, tn=t.down_tn, tk=t.down_tk, norm_gain=attn_norm_gain[l + 1],
                cvts=(_Cvt(w_in, l + 1), _Cvt(w_branch, l + 1), _Cvt(w_out, l + 1)), name="mlp_down")
            nx = _Normed(xg, ss)
    return xf.reshape(batch, seq, d).astype(x.dtype)
```

```python
import functools
import math
from typing import Callable, NamedTuple

import numpy as np
import jax
import jax.numpy as jnp
from jax import lax
from jax.experimental import pallas as pl
from jax.experimental.pallas import tpu as pltpu

ATTN_HEADS = 32
ATTN_KV_HEADS = 4
ATTN_HEAD_DIM = 64
ATTN_GROUP = ATTN_HEADS // ATTN_KV_HEADS
ATTN_BLOCK = 128
ATTN_SCALE = ATTN_HEAD_DIM ** -0.5
NUM_BUCKETS = 32
MAX_DISTANCE = 128
HG_HEADS = 8
HG_DK = 128
HG_DV = 128
HG_CHUNK = 64
EPS = 1e-6

ATTN_Q_DIM = ATTN_HEADS * ATTN_HEAD_DIM
ATTN_KV_DIM = ATTN_KV_HEADS * ATTN_HEAD_DIM
QKV_DIM = ATTN_Q_DIM + 2 * ATTN_KV_DIM
HG_DIM = HG_HEADS * HG_DK
HGRN_IN_DIM = 4 * HG_DIM

V7X_LANES = 128
V7X_F32_SUBLANES = 8
BF16_SUBLANES = 16
V7X_VMEM_LIMIT_BYTES = 56 * 1024 * 1024

_BF16 = jnp.bfloat16
_F32 = jnp.float32


class _Tiles(NamedTuple):
    tm: int = 1024
    qkv_tn: int = 1280
    proj_tn: int = 512
    merge_tn: int = 256
    up_tn: int = 1024
    down_tn: int = 1024
    down_tk: int = 2048


_TILES = _Tiles()


def _nt_dot(a, b):
    return lax.dot_general(a, b, (((1,), (1,)), ((), ())), preferred_element_type=_F32)


def _tn_dot(a, b):
    return lax.dot_general(a, b, (((0,), (0,)), ((), ())), preferred_element_type=_F32)


def _dot(a, b):
    return jnp.dot(a, b, preferred_element_type=_F32)


def _split3_bf16(x):
    a1 = x.astype(_BF16)
    r1 = x - a1.astype(_F32)
    a2 = r1.astype(_BF16)
    a3 = (r1 - a2.astype(_F32)).astype(_BF16)
    return a1, a2, a3


def _compiler_params(n_grid):
    return pltpu.CompilerParams(
        dimension_semantics=("arbitrary",) * n_grid,
        vmem_limit_bytes=V7X_VMEM_LIMIT_BYTES,
    )


class _Normed(NamedTuple):
    xg: jax.Array
    ss: jax.Array


def _emit_normed(y, first_col_tile, gain_ref, xg_ref, ss_ref):
    xg_ref[...] = (y * gain_ref[...]).astype(xg_ref.dtype)
    part = jnp.broadcast_to(jnp.sum(y * y, axis=1, keepdims=True), ss_ref.shape)

    @pl.when(first_col_tile)
    def _():
        ss_ref[...] = part

    @pl.when(jnp.logical_not(first_col_tile))
    def _():
        ss_ref[...] += part


def _scale_rows_rms(y, ss_ref, d):
    r = lax.rsqrt(ss_ref[...] * (1.0 / d) + EPS)
    return jnp.concatenate([y[:, c * V7X_LANES:(c + 1) * V7X_LANES] * r for c in range(y.shape[1] // V7X_LANES)],
                           axis=1)


def _normed_input_kernel(x_ref, g_ref, xg_ref, ss_ref):
    _emit_normed(x_ref[...], True, g_ref, xg_ref, ss_ref)


def _normed_input(x, gain, *, rows=256):
    m, d = x.shape
    rows = min(rows, m)
    xg, ss = pl.pallas_call(
        _normed_input_kernel,
        grid=(m // rows,),
        in_specs=[pl.BlockSpec((rows, d), lambda i: (i, 0)),
                  pl.BlockSpec((1, d), lambda i: (0, 0))],
        out_specs=[pl.BlockSpec((rows, d), lambda i: (i, 0)),
                   pl.BlockSpec((rows, V7X_LANES), lambda i: (i, 0))],
        out_shape=[jax.ShapeDtypeStruct((m, d), _BF16), jax.ShapeDtypeStruct((m, V7X_LANES), _F32)],
        compiler_params=_compiler_params(1),
        name="normed_input",
    )(x, gain.reshape(1, d).astype(_F32))
    return _Normed(xg, ss)


class _Cvt(NamedTuple):
    stacked: jax.Array
    layer: int


def _cvt_plumbing(cvts, n_steps, linear_step):
    in_specs, out_specs, out_shapes, args = [], [], [], []
    for c in cvts:
        _, rows, cols = c.stacked.shape
        units = rows // BF16_SUBLANES
        assert rows % BF16_SUBLANES == 0
        blocks = max(b for b in range(1, min(units, n_steps) + 1) if units % b == 0)
        brows = rows // blocks
        idx = functools.partial(lambda *g, blocks: (linear_step(*g) * blocks) // n_steps, blocks=blocks)
        in_specs.append(pl.BlockSpec((None, brows, cols),
                                     functools.partial(lambda *g, idx, layer: (layer, idx(*g), 0), idx=idx, layer=c.layer)))
        out_specs.append(pl.BlockSpec((brows, cols), functools.partial(lambda *g, idx: (idx(*g), 0), idx=idx)))
        out_shapes.append(jax.ShapeDtypeStruct((rows, cols), _BF16))
        args.append(c.stacked)
    return in_specs, out_specs, out_shapes, args


def _run_cvts(src_refs, dst_refs):
    for s, d in zip(src_refs, dst_refs):
        d[...] = s[...].astype(d.dtype)


def _relu2(y):
    r = jnp.maximum(y, 0.0)
    return r * r


def _mm_kernel(*refs, epilogue, has_row_ss, has_residual, has_norm, n_cvt):
    refs = list(refs)
    a_ref, w_ref = refs.pop(0), refs.pop(0)
    ss_in_ref = refs.pop(0) if has_row_ss else None
    r_ref = refs.pop(0) if has_residual else None
    gain_ref = refs.pop(0) if has_norm else None
    cvt_src = [refs.pop(0) for _ in range(n_cvt)]
    o_ref = refs.pop(0)
    xg_ref, ss_ref = (refs.pop(0), refs.pop(0)) if has_norm else (None, None)
    cvt_dst = refs
    y = _dot(a_ref[...], w_ref[...])
    if has_row_ss:
        y = _scale_rows_rms(y, ss_in_ref, a_ref.shape[1])
    if epilogue is not None:
        y = epilogue(y)
    if has_residual:
        y = r_ref[...] + y
    o_ref[...] = y.astype(o_ref.dtype)
    _run_cvts(cvt_src, cvt_dst)
    if has_norm:
        _emit_normed(y, pl.program_id(1) == 0, gain_ref, xg_ref, ss_ref)


def _matmul(a, w, *, n_out, col_block0=0, tm, tn, out_dtype, row_ss=None, epilogue=None, residual=None,
            norm_gain=None, cvts=(), name):
    m, k = a.shape
    tm, tn = min(tm, m), min(tn, n_out)
    grid = (m // tm, n_out // tn)
    n_steps = grid[0] * grid[1]
    in_specs = [pl.BlockSpec((tm, k), lambda i, j: (i, 0)),
                pl.BlockSpec((k, tn), lambda i, j: (0, j + col_block0))]
    args = [a, w]
    out_specs = [pl.BlockSpec((tm, tn), lambda i, j: (i, j))]
    out_shapes = [jax.ShapeDtypeStruct((m, n_out), out_dtype)]
    if row_ss is not None:
        in_specs.append(pl.BlockSpec((tm, V7X_LANES), lambda i, j: (i, 0)))
        args.append(row_ss)
    if residual is not None:
        in_specs.append(pl.BlockSpec((tm, tn), lambda i, j: (i, j)))
        args.append(residual)
    if norm_gain is not None:
        in_specs.append(pl.BlockSpec((1, tn), lambda i, j: (0, j)))
        args.append(norm_gain.reshape(1, n_out).astype(_F32))
        out_specs += [pl.BlockSpec((tm, tn), lambda i, j: (i, j)),
                      pl.BlockSpec((tm, V7X_LANES), lambda i, j: (i, 0))]
        out_shapes += [jax.ShapeDtypeStruct((m, n_out), _BF16), jax.ShapeDtypeStruct((m, V7X_LANES), _F32)]
    c_in, c_out, c_shapes, c_args = _cvt_plumbing(cvts, n_steps, lambda i, j: i * grid[1] + j)
    return pl.pallas_call(
        functools.partial(_mm_kernel, epilogue=epilogue, has_row_ss=row_ss is not None,
                          has_residual=residual is not None, has_norm=norm_gain is not None, n_cvt=len(cvts)),
        grid=grid,
        in_specs=in_specs + c_in,
        out_specs=out_specs + c_out,
        out_shape=out_shapes + c_shapes,
        compiler_params=_compiler_params(2),
        name=name,
    )(*args, *c_args)


def _mm_ktiled_res_kernel(*refs, has_norm, n_cvt):
    refs = list(refs)
    a_ref, w_ref, r_ref = refs.pop(0), refs.pop(0), refs.pop(0)
    gain_ref = refs.pop(0) if has_norm else None
    cvt_src = [refs.pop(0) for _ in range(n_cvt)]
    o_ref = refs.pop(0)
    xg_ref, ss_ref = (refs.pop(0), refs.pop(0)) if has_norm else (None, None)
    cvt_dst = refs

    @pl.when(pl.program_id(2) == 0)
    def _():
        o_ref[...] = r_ref[...] + _dot(a_ref[...], w_ref[...])
        _run_cvts(cvt_src, cvt_dst)

    @pl.when(pl.program_id(2) > 0)
    def _():
        o_ref[...] += _dot(a_ref[...], w_ref[...])
        _run_cvts(cvt_src, cvt_dst)

    if has_norm:
        @pl.when(pl.program_id(2) == pl.num_programs(2) - 1)
        def _():
            _emit_normed(o_ref[...], pl.program_id(1) == 0, gain_ref, xg_ref, ss_ref)


def _matmul_ktiled_res(a, w, residual, *, tm, tn, tk, norm_gain=None, cvts=(), name):
    m, k = a.shape
    _, n = w.shape
    tm, tn, tk = min(tm, m), min(tn, n), min(tk, k)
    grid = (m // tm, n // tn, k // tk)
    n_steps = grid[0] * grid[1] * grid[2]
    in_specs = [pl.BlockSpec((tm, tk), lambda i, j, q: (i, q)),
                pl.BlockSpec((tk, tn), lambda i, j, q: (q, j)),
                pl.BlockSpec((tm, tn), lambda i, j, q: (i, j))]
    args = [a, w, residual]
    out_specs = [pl.BlockSpec((tm, tn), lambda i, j, q: (i, j))]
    out_shapes = [jax.ShapeDtypeStruct((m, n), _F32)]
    if norm_gain is not None:
        in_specs.append(pl.BlockSpec((1, tn), lambda i, j, q: (0, j)))
        args.append(norm_gain.reshape(1, n).astype(_F32))
        out_specs += [pl.BlockSpec((tm, tn), lambda i, j, q: (i, j)),
                      pl.BlockSpec((tm, V7X_LANES), lambda i, j, q: (i, 0))]
        out_shapes += [jax.ShapeDtypeStruct((m, n), _BF16), jax.ShapeDtypeStruct((m, V7X_LANES), _F32)]
    c_in, c_out, c_shapes, c_args = _cvt_plumbing(
        cvts, n_steps, lambda i, j, q: (i * grid[1] + j) * grid[2] + q)
    return pl.pallas_call(
        functools.partial(_mm_ktiled_res_kernel, has_norm=norm_gain is not None, n_cvt=len(cvts)),
        grid=grid,
        in_specs=in_specs + c_in,
        out_specs=out_specs + c_out,
        out_shape=out_shapes + c_shapes,
        compiler_params=_compiler_params(3),
        name=name,
    )(*args, *c_args)


def _merge_kernel(*refs, n_cvt):
    h_ref, ss_ref, oa_ref, oh_ref, wga_ref, wgh_ref, wa_ref, wh_ref = refs[:8]
    cvt_src = refs[8:8 + n_cvt]
    o_ref = refs[8 + n_cvt]
    cvt_dst = refs[9 + n_cvt:]
    h = h_ref[...]
    d = h_ref.shape[1]
    ga = _scale_rows_rms(_dot(h, wga_ref[...]), ss_ref, d)
    gh = _scale_rows_rms(_dot(h, wgh_ref[...]), ss_ref, d)
    ba = _dot(oa_ref[...], wa_ref[...])
    bh = _dot(oh_ref[...], wh_ref[...])
    o_ref[...] = (jax.nn.sigmoid(ga) * ba + jax.nn.sigmoid(gh) * bh).astype(o_ref.dtype)
    _run_cvts(cvt_src, cvt_dst)


def _merge(nx, o_attn, o_hgrn, w_in_b, w_branch_b, *, gate_col0, tm, tn, cvts=()):
    m, d = nx.xg.shape
    tm, tn = min(tm, m), min(tn, d)
    n_tiles = d // tn
    assert ATTN_Q_DIM % HG_DIM == 0 and gate_col0 % tn == 0
    g0 = gate_col0 // tn
    grid = (m // tm, n_tiles)
    c_in, c_out, c_shapes, c_args = _cvt_plumbing(cvts, grid[0] * grid[1], lambda i, j: i * n_tiles + j)
    return pl.pallas_call(
        functools.partial(_merge_kernel, n_cvt=len(cvts)),
        grid=grid,
        in_specs=[
            pl.BlockSpec((tm, d), lambda i, j: (i, 0)),
            pl.BlockSpec((tm, V7X_LANES), lambda i, j: (i, 0)),
            pl.BlockSpec((tm, ATTN_Q_DIM), lambda i, j: (i, 0)),
            pl.BlockSpec((tm, HG_DIM), lambda i, j: (i, 0)),
            pl.BlockSpec((d, tn), lambda i, j: (0, g0 + j)),
            pl.BlockSpec((d, tn), lambda i, j: (0, g0 + n_tiles + j)),
            pl.BlockSpec((ATTN_Q_DIM, tn), lambda i, j: (0, j)),
            pl.BlockSpec((HG_DIM, tn), lambda i, j: (ATTN_Q_DIM // HG_DIM, j)),
        ] + c_in,
        out_specs=[pl.BlockSpec((tm, tn), lambda i, j: (i, j))] + c_out,
        out_shape=[jax.ShapeDtypeStruct((m, d), _BF16)] + c_shapes,
        compiler_params=_compiler_params(2),
        name="gated_merge",
    )(nx.xg, nx.ss, o_attn, o_hgrn, w_in_b, w_in_b, w_branch_b, w_branch_b, *c_args)


def _bucket_table():
    ki = np.arange(2 * ATTN_BLOCK)[:, None]
    qi = np.arange(ATTN_BLOCK)[None, :]
    dist = qi + ATTN_BLOCK - ki
    in_window = (dist >= 0) & (dist < ATTN_BLOCK)
    in_window_first = in_window & (ki >= ATTN_BLOCK)
    dist = np.maximum(dist, 0)
    max_exact = NUM_BUCKETS // 2
    d = np.maximum(dist, 1).astype(np.float64)
    val = np.log(d / max_exact) / math.log(MAX_DISTANCE / max_exact) * (NUM_BUCKETS - max_exact)
    large = max_exact + np.trunc(val).astype(np.int64)
    frac = np.abs(val - np.round(val))[in_window & (dist > max_exact)]
    assert frac.min() > 1e-3
    bucket = np.where(dist < max_exact, dist, np.minimum(large, NUM_BUCKETS - 1))
    return bucket.reshape(-1), in_window_first.reshape(-1), in_window.reshape(-1)


def _bias_kernel(rbt_ref, onehot_ref, mask_ref, o_ref):
    a1, a2, a3 = _split3_bf16(rbt_ref[...])
    lhs = jnp.concatenate([a1, a2, a3], axis=1)
    base = _dot(lhs, onehot_ref[...])
    o_ref[0] = base + mask_ref[0]
    o_ref[1] = base + mask_ref[1]


def _bias_band(rel_bias, *, tn=4096):
    bucket, first, other = _bucket_table()
    n = bucket.shape[0]
    onehot = (np.arange(NUM_BUCKETS)[:, None] == bucket[None, :]).astype(np.float32)
    onehot3 = jnp.asarray(np.concatenate([onehot] * 3, axis=0), dtype=_BF16)
    maskadd = jnp.asarray(np.where(np.stack([first, other])[:, None, :], 0.0, -np.inf), dtype=_F32)
    out = pl.pallas_call(
        _bias_kernel,
        grid=(n // tn,),
        in_specs=[pl.BlockSpec((ATTN_HEADS, NUM_BUCKETS), lambda j: (0, 0)),
                  pl.BlockSpec((3 * NUM_BUCKETS, tn), lambda j: (0, j)),
                  pl.BlockSpec((2, 1, tn), lambda j: (0, 0, j))],
        out_specs=pl.BlockSpec((2, ATTN_HEADS, tn), lambda j: (0, 0, j)),
        out_shape=jax.ShapeDtypeStruct((2, ATTN_HEADS, n), _F32),
        compiler_params=_compiler_params(1),
        name="rel_bias_band",
    )(rel_bias.astype(_F32).T, onehot3, maskadd)
    return out.reshape(2, ATTN_HEADS, 2 * ATTN_BLOCK, ATTN_BLOCK)


def _group_rms(x, blockdiag, gain):
    x2 = x * x
    hi = x2.astype(_BF16)
    lo = (x2 - hi.astype(_F32)).astype(_BF16)
    ss = _dot(hi, blockdiag) + _dot(lo, blockdiag)
    return x * lax.rsqrt(ss * (1.0 / ATTN_HEAD_DIM) + EPS) * gain


def _pair_stack(slab, low_half):
    swapped = pltpu.roll(slab, ATTN_HEAD_DIM, 1)
    zero = jnp.zeros_like(slab)
    even = jnp.concatenate([jnp.where(low_half, slab, zero), jnp.where(low_half, zero, swapped)], axis=0)
    odd = jnp.concatenate([jnp.where(low_half, swapped, zero), jnp.where(low_half, zero, slab)], axis=0)
    return even.astype(_BF16), odd.astype(_BF16)


class _AttnSteps(NamedTuple):
    prologue: Callable[[], None]
    scores: Callable[[int], None]
    finish: Callable[[int], None]
    n_pairs: int


def _attn_steps(sink_ref, q_ref, kvc_ref, kvp_ref, bias_ref, gq_ref, gk_ref, bd_ref, o_ref):
    w = 2 * ATTN_BLOCK
    st = {}

    def prologue():
        bd = bd_ref[...]
        kv = jnp.concatenate([kvp_ref[...], kvc_ref[...]], axis=0)
        kn = _group_rms(kv[:, :ATTN_KV_DIM], bd, gk_ref[...])
        v = kv[:, ATTN_KV_DIM:]
        lane = lax.broadcasted_iota(jnp.int32, (w, V7X_LANES), 1)
        low_half = lane < ATTN_HEAD_DIM
        k_stacks, v_stacks = [], []
        for s in range(ATTN_KV_DIM // V7X_LANES):
            k_stacks.extend(_pair_stack(kn[:, s * V7X_LANES:(s + 1) * V7X_LANES], low_half))
            v_stacks.extend(_pair_stack(v[:, s * V7X_LANES:(s + 1) * V7X_LANES], low_half))
        st["k"], st["v"], st["bd"] = k_stacks, v_stacks, bd

    def kv_head(pair):
        return (pair * V7X_LANES) // (ATTN_GROUP * ATTN_HEAD_DIM)

    def scores(pair):
        s, half = divmod(pair, 2)
        if half == 0:
            q = q_ref[:, s * w:(s + 1) * w]
            st["qn"] = (_group_rms(q, st["bd"], gq_ref[...]) * ATTN_SCALE).astype(_BF16)
        qn = st["qn"][:, half * V7X_LANES:(half + 1) * V7X_LANES]
        st["sc", pair] = _nt_dot(st["k"][kv_head(pair)], qn)

    def finish(pair):
        sc = st.pop(("sc", pair))
        probs = []
        for e in range(2):
            head = 2 * pair + e
            se = sc[e * w:(e + 1) * w] + bias_ref[head]
            sink = sink_ref[head]
            mx = jnp.maximum(jnp.max(se, axis=0, keepdims=True), sink)
            p = jnp.exp(se - mx)
            denom = jnp.sum(p, axis=0, keepdims=True) + jnp.exp(sink - mx)
            probs.append((p * (1.0 / denom)).astype(_BF16))
        o = _tn_dot(jnp.concatenate(probs, axis=0), st["v"][kv_head(pair)])
        o_ref[:, pair * V7X_LANES:(pair + 1) * V7X_LANES] = o.astype(o_ref.dtype)

    return _AttnSteps(prologue, scores, finish, ATTN_HEADS // 2)


def _proj_attn_kernel(a_ref, w_ref, ss_ref, sink_ref, q_ref, kvc_ref, kvp_ref, bias_ref, gq_ref, gk_ref, bd_ref,
                      hg_ref, o_ref):
    attn = _attn_steps(sink_ref, q_ref, kvc_ref, kvp_ref, bias_ref, gq_ref, gk_ref, bd_ref, o_ref)
    kc = a_ref.shape[1] // attn.n_pairs
    attn.prologue()
    attn.scores(0)
    for p in range(attn.n_pairs):
        part = _dot(a_ref[:, p * kc:(p + 1) * kc], w_ref[p * kc:(p + 1) * kc, :])
        if p + 1 < attn.n_pairs:
            attn.scores(p + 1)
        if p == 0:
            hg_ref[...] = part
        elif p + 1 < attn.n_pairs:
            hg_ref[...] += part
        else:
            hg_ref[...] = _scale_rows_rms(hg_ref[...] + part, ss_ref, a_ref.shape[1])
        attn.finish(p)


def _proj_hgrn_with_attention(nx, w_in_b, col_block0, qkv, sinks, bias, q_gain, k_gain, batch, seq, *, tm, tn):
    m, k = nx.xg.shape
    tm = min(tm, m)
    nb = seq // ATTN_BLOCK
    grid = (m // tm, HGRN_IN_DIM // tn)
    nj = grid[1]
    assert grid[0] * nj == batch * nb and k % (ATTN_HEADS // 2) == 0
    kv_col = ATTN_Q_DIM // (2 * ATTN_KV_DIM)
    assert ATTN_Q_DIM % (2 * ATTN_KV_DIM) == 0
    blockdiag = jnp.asarray(np.kron(np.eye(2 * ATTN_BLOCK // ATTN_HEAD_DIM), np.ones((ATTN_HEAD_DIM,) * 2)), _BF16)
    tile = lambda g: jnp.tile(g.reshape(1, ATTN_HEAD_DIM).astype(_F32), (1, 2 * ATTN_BLOCK // ATTN_HEAD_DIM))
    blk = lambda i, j: i * nj + j
    first = lambda i, j: (blk(i, j) % nb) == 0
    return pl.pallas_call(
        _proj_attn_kernel,
        grid=grid,
        in_specs=[
            pl.BlockSpec((tm, k), lambda i, j: (i, 0)),
            pl.BlockSpec((k, tn), lambda i, j: (0, j + col_block0)),
            pl.BlockSpec((tm, V7X_LANES), lambda i, j: (i, 0)),
            pl.BlockSpec(memory_space=pltpu.SMEM),
            pl.BlockSpec((ATTN_BLOCK, ATTN_Q_DIM), lambda i, j: (blk(i, j), 0)),
            pl.BlockSpec((ATTN_BLOCK, 2 * ATTN_KV_DIM), lambda i, j: (blk(i, j), kv_col)),
            pl.BlockSpec((ATTN_BLOCK, 2 * ATTN_KV_DIM),
                         lambda i, j: (jnp.where(first(i, j), blk(i, j), blk(i, j) - 1), kv_col)),
            pl.BlockSpec((None, ATTN_HEADS, 2 * ATTN_BLOCK, ATTN_BLOCK),
                         lambda i, j: (jnp.where(first(i, j), 0, 1), 0, 0, 0)),
            pl.BlockSpec((1, 2 * ATTN_BLOCK), lambda i, j: (0, 0)),
            pl.BlockSpec((1, 2 * ATTN_BLOCK), lambda i, j: (0, 0)),
            pl.BlockSpec((2 * ATTN_BLOCK, 2 * ATTN_BLOCK), lambda i, j: (0, 0)),
        ],
        out_specs=[pl.BlockSpec((tm, tn), lambda i, j: (i, j)),
                   pl.BlockSpec((ATTN_BLOCK, ATTN_Q_DIM), lambda i, j: (blk(i, j), 0))],
        out_shape=[jax.ShapeDtypeStruct((m, HGRN_IN_DIM), _F32),
                   jax.ShapeDtypeStruct((m, ATTN_Q_DIM), _BF16)],
        compiler_params=_compiler_params(2),
        name="proj_hgrn_attn",
    )(nx.xg, w_in_b, nx.ss, sinks.astype(_F32), qkv, qkv, qkv, bias, tile(q_gain), tile(k_gain), blockdiag)


_HG_LEVELS = tuple(HG_CHUNK >> (i + 1) for i in range(int(math.log2(HG_CHUNK))))
_HG_PIPELINE_LEAD = 3
_HG_MATMUL_LEVELS = tuple(m for m in _HG_LEVELS if m < V7X_F32_SUBLANES)


def _hgrn_tables():
    c = HG_CHUNK
    t = np.arange(c)
    blocks = [(t[None, :] <= t[:, None])]
    masks = [np.eye(c, dtype=bool)]
    for m in _HG_LEVELS:
        ref = (t // (2 * m)) * (2 * m) + m - 1
        upper = (t % (2 * m)) >= m
        if m in _HG_MATMUL_LEVELS:
            up_rows = upper[:, None] & (t[None, :] > ref[:, None]) & (t[None, :] <= t[:, None])
            lo_rows = (~upper)[:, None] & (t[None, :] > t[:, None]) & (t[None, :] <= ref[:, None])
            blocks.append(up_rows | lo_rows)
        same = (t[:, None] // (2 * m)) == (t[None, :] // (2 * m))
        masks.append(same & upper[:, None] & (~upper)[None, :])
    sums = np.concatenate(blocks, axis=0).astype(np.float32)
    return np.concatenate([sums] * 3, axis=1), np.stack(masks).astype(np.float32)


def _hgrn_kernel(p_ref, lbl_ref, gain_ref, sums_ref, masks_ref, o_ref, state_ref, *, layer, rows):
    c = HG_CHUNK

    @pl.when(pl.program_id(1) == 0)
    def _():
        state_ref[...] = jnp.zeros_like(state_ref)

    lg = lbl_ref[...]
    e = jnp.exp(lg - jnp.max(lg, axis=0, keepdims=True))
    sm = e / jnp.sum(e, axis=0, keepdims=True)
    lb_all = jnp.zeros_like(sm[0:1])
    for i in range(1, layer + 1):
        lb_all = lb_all + sm[i:i + 1]

    row = lax.broadcasted_iota(jnp.int32, (c, 1), 0)
    gain = gain_ref[...]
    sums = sums_ref[...]

    def chunk_body(ci, carry):
        r0 = pl.multiple_of(ci * c, c)
        st = [dict() for _ in range(HG_HEADS)]

        def seg(h, which):
            return p_ref[pl.ds(r0, c), which * HG_DIM + h * HG_DK:which * HG_DIM + (h + 1) * HG_DK]

        def gates(h):
            lb = lb_all[:, h * HG_DK:(h + 1) * HG_DK]
            hq = seg(h, 0)
            forget = lb + (1.0 - lb) * jax.nn.sigmoid(seg(h, 1))
            d = st[h]
            d["kk"] = 1.0 - forget
            d["qf"] = hq * jax.nn.sigmoid(hq)
            d["expo"] = _dot(sums, jnp.concatenate(_split3_bf16(jnp.log(forget)), axis=0))

        def level_decay(expo, m):
            if m in _HG_MATMUL_LEVELS:
                k = 1 + _HG_MATMUL_LEVELS.index(m)
                return jnp.exp(expo[k * c:(k + 1) * c])
            bcum = expo[0:c]
            blocks = []
            for s0 in range(0, c, 2 * m):
                blocks.append(-jnp.abs(bcum[s0:s0 + 2 * m] - bcum[s0 + m - 1:s0 + m]))
            return jnp.exp(jnp.concatenate(blocks, axis=0))

        def products(h):
            d = st[h]
            qf, kk = d.pop("qf"), d.pop("kk")
            d["v"] = seg(h, 2).astype(_BF16)
            expo = d.pop("expo")
            bcum = expo[0:c]
            e_cum = jnp.exp(bcum)
            e_end = jnp.exp(bcum[c - 1:c] - bcum)
            state_t = state_ref[h]
            d["o_inter"] = _nt_dot((qf * e_cum).astype(_BF16), state_t.astype(_BF16))
            parts = [_nt_dot(qf.astype(_BF16), kk.astype(_BF16))]
            for m in _HG_LEVELS:
                upper = (row & m) != 0
                z = (jnp.where(upper, qf, kk) * level_decay(expo, m)).astype(_BF16)
                parts.append(_nt_dot(z, z))
            d["parts"] = parts
            state_ref[h] = state_t * e_cum[c - 1:c] + _tn_dot(d["v"], (kk * e_end).astype(_BF16))

        def output(h):
            d = st[h]
            parts = d.pop("parts")
            scores = parts[0] * masks_ref[0]
            for li in range(len(_HG_LEVELS)):
                scores = scores + parts[1 + li] * masks_ref[1 + li]
            o = d.pop("o_inter") + _dot(scores.astype(_BF16), d.pop("v"))
            hg = seg(h, 3)
            y = o * lax.rsqrt(jnp.mean(o * o, axis=-1, keepdims=True) + EPS) * gain
            y = y * (hg * jax.nn.sigmoid(hg))
            o_ref[pl.ds(r0, c), h * HG_DV:(h + 1) * HG_DV] = y.astype(o_ref.dtype)

        lead = _HG_PIPELINE_LEAD
        for h in range(-2 * lead, HG_HEADS):
            if 0 <= h + 2 * lead < HG_HEADS:
                gates(h + 2 * lead)
            if 0 <= h + lead < HG_HEADS:
                products(h + lead)
            if 0 <= h:
                output(h)
        return carry

    lax.fori_loop(0, rows // c, chunk_body, 0)


def _hgrn(p, lb_logits, norm_gain, layer, batch, seq, *, rows=256):
    rows = min(rows, seq)
    steps = seq // rows
    depth = lb_logits.shape[0]
    sums, masks = _hgrn_tables()
    return pl.pallas_call(
        functools.partial(_hgrn_kernel, layer=layer, rows=rows),
        grid=(batch, steps),
        in_specs=[
            pl.BlockSpec((rows, HGRN_IN_DIM), lambda b, s: (b * steps + s, 0)),
            pl.BlockSpec((depth, HG_DIM), lambda b, s: (0, 0)),
            pl.BlockSpec((1, HG_DV), lambda b, s: (0, 0)),
            pl.BlockSpec(sums.shape, lambda b, s: (0, 0)),
            pl.BlockSpec(masks.shape, lambda b, s: (0, 0, 0)),
        ],
        out_specs=pl.BlockSpec((rows, HG_DIM), lambda b, s: (b * steps + s, 0)),
        out_shape=jax.ShapeDtypeStruct((batch * seq, HG_DIM), _BF16),
        scratch_shapes=[pltpu.VMEM((HG_HEADS, HG_DV, HG_DK), _F32)],
        compiler_params=_compiler_params(2),
        name="hgrn2_scan",
    )(p, lb_logits.astype(_F32), norm_gain.reshape(1, HG_DV).astype(_F32),
      jnp.asarray(sums, _BF16), jnp.asarray(masks, _F32))


def kernel(x, attn_norm_gain, w_in, q_norm_gain, k_norm_gain, attn_sinks, rel_bias,
           hgrn_lb_logits, hgrn_norm_gain, w_branch, w_out, mlp_norm_gain, w_up, w_down):
    batch, seq, d = x.shape
    depth = w_in.shape[0]
    m = batch * seq
    xf = x.reshape(m, d).astype(_F32)
    bias = _bias_band(rel_bias)
    t = _TILES
    hg_col0 = QKV_DIM // t.proj_tn
    gate_col0 = QKV_DIM + HGRN_IN_DIM
    assert QKV_DIM % t.proj_tn == 0

    w_in_b = w_in[0].astype(_BF16)
    w_br_b = w_branch[0].astype(_BF16)
    w_o_b = w_out[0].astype(_BF16)
    nx = _normed_input(xf, attn_norm_gain[0])
    for l in range(depth):
        qkv, = _matmul(nx.xg, w_in_b, n_out=QKV_DIM, tm=t.tm, tn=t.qkv_tn, out_dtype=_F32, row_ss=nx.ss,
                       name="proj_qkv")
        hgp, o_attn = _proj_hgrn_with_attention(nx, w_in_b, hg_col0, qkv, attn_sinks[l], bias, q_norm_gain[l],
                                                k_norm_gain[l], batch, seq, tm=t.tm, tn=t.proj_tn)
        o_hgrn = _hgrn(hgp, hgrn_lb_logits, hgrn_norm_gain[l], l, batch, seq)
        merged, w_u_b = _merge(nx, o_attn, o_hgrn, w_in_b, w_br_b, gate_col0=gate_col0,
                               tm=t.tm, tn=t.merge_tn, cvts=(_Cvt(w_up, l),))
        xf, xg, ss = _matmul(merged, w_o_b, n_out=d, tm=t.tm, tn=t.proj_tn, out_dtype=_F32, residual=xf,
                             norm_gain=mlp_norm_gain[l], name="out_proj")

        u, w_d_b = _matmul(xg, w_u_b, n_out=w_up.shape[2], tm=t.tm, tn=t.up_tn, out_dtype=_BF16,
                           row_ss=ss, epilogue=_relu2, cvts=(_Cvt(w_down, l),), name="mlp_up")
        if l + 1 == depth:
            xf, = _matmul_ktiled_res(u, w_d_b, xf, tm=t.tm, tn=t.down_tn, tk=t.down_tk, name="mlp_down")
        else:
            xf, xg, ss, w_in_b, w_br_b, w_o_b = _matmul_ktiled_res(
                u, w_d_b, xf, tm=t.tm, tn=t.down_tn, tk=t.down_tk, norm_gain=attn_norm_gain[l + 1],
                cvts=(_Cvt(w_in, l + 1), _Cvt(w_branch, l + 1), _Cvt(w_out, l + 1)), name="mlp_down")
            nx = _Normed(xg, ss)
    return xf.reshape(batch, seq, d).astype(x.dtype)
```

```python
import functools
import math
from typing import Callable, NamedTuple

import numpy as np
import jax
import jax.numpy as jnp
from jax import lax
from jax.experimental import pallas as pl
from jax.experimental.pallas import tpu as pltpu

ATTN_HEADS = 32
ATTN_KV_HEADS = 4
ATTN_HEAD_DIM = 64
ATTN_GROUP = ATTN_HEADS // ATTN_KV_HEADS
ATTN_BLOCK = 128
ATTN_SCALE = ATTN_HEAD_DIM ** -0.5
NUM_BUCKETS = 32
MAX_DISTANCE = 128
HG_HEADS = 8
HG_DK = 128
HG_DV = 128
HG_CHUNK = 64
EPS = 1e-6

ATTN_Q_DIM = ATTN_HEADS * ATTN_HEAD_DIM
ATTN_KV_DIM = ATTN_KV_HEADS * ATTN_HEAD_DIM
QKV_DIM = ATTN_Q_DIM + 2 * ATTN_KV_DIM
HG_DIM = HG_HEADS * HG_DK
HGRN_IN_DIM = 4 * HG_DIM

V7X_LANES = 128
V7X_F32_SUBLANES = 8
BF16_SUBLANES = 16
V7X_VMEM_LIMIT_BYTES = 56 * 1024 * 1024

_BF16 = jnp.bfloat16
_F32 = jnp.float32


class _Tiles(NamedTuple):
    tm: int = 1024
    qkv_tn: int = 1280
    proj_tn: int = 512
    merge_tn: int = 256
    up_tn: int = 1024
    down_tn: int = 1024
    down_tk: int = 2048


_TILES = _Tiles()


def _nt_dot(a, b):
    return lax.dot_general(a, b, (((1,), (1,)), ((), ())), preferred_element_type=_F32)


def _tn_dot(a, b):
    return lax.dot_general(a, b, (((0,), (0,)), ((), ())), preferred_element_type=_F32)


def _dot(a, b):
    return jnp.dot(a, b, preferred_element_type=_F32)


def _split3_bf16(x):
    a1 = x.astype(_BF16)
    r1 = x - a1.astype(_F32)
    a2 = r1.astype(_BF16)
    a3 = (r1 - a2.astype(_F32)).astype(_BF16)
    return a1, a2, a3


def _compiler_params(n_grid):
    return pltpu.CompilerParams(
        dimension_semantics=("arbitrary",) * n_grid,
        vmem_limit_bytes=V7X_VMEM_LIMIT_BYTES,
    )


class _Normed(NamedTuple):
    xg: jax.Array
    ss: jax.Array


def _emit_normed(y, first_col_tile, gain_ref, xg_ref, ss_ref):
    xg_ref[...] = (y * gain_ref[...]).astype(xg_ref.dtype)
    part = jnp.broadcast_to(jnp.sum(y * y, axis=1, keepdims=True), ss_ref.shape)

    @pl.when(first_col_tile)
    def _():
        ss_ref[...] = part

    @pl.when(jnp.logical_not(first_col_tile))
    def _():
        ss_ref[...] += part


def _scale_rows_rms(y, ss_ref, d):
    r = lax.rsqrt(ss_ref[...] * (1.0 / d) + EPS)
    return jnp.concatenate([y[:, c * V7X_LANES:(c + 1) * V7X_LANES] * r for c in range(y.shape[1] // V7X_LANES)],
                           axis=1)


def _normed_input_kernel(x_ref, g_ref, xg_ref, ss_ref):
    _emit_normed(x_ref[...], True, g_ref, xg_ref, ss_ref)


def _normed_input(x, gain, *, rows=256):
    m, d = x.shape
    rows = min(rows, m)
    xg, ss = pl.pallas_call(
        _normed_input_kernel,
        grid=(m // rows,),
        in_specs=[pl.BlockSpec((rows, d), lambda i: (i, 0)),
                  pl.BlockSpec((1, d), lambda i: (0, 0))],
        out_specs=[pl.BlockSpec((rows, d), lambda i: (i, 0)),
                   pl.BlockSpec((rows, V7X_LANES), lambda i: (i, 0))],
        out_shape=[jax.ShapeDtypeStruct((m, d), _BF16), jax.ShapeDtypeStruct((m, V7X_LANES), _F32)],
        compiler_params=_compiler_params(1),
        name="normed_input",
    )(x, gain.reshape(1, d).astype(_F32))
    return _Normed(xg, ss)


class _Cvt(NamedTuple):
    stacked: jax.Array
    layer: int


def _cvt_plumbing(cvts, n_steps, linear_step):
    in_specs, out_specs, out_shapes, args = [], [], [], []
    for c in cvts:
        _, rows, cols = c.stacked.shape
        units = rows // BF16_SUBLANES
        assert rows % BF16_SUBLANES == 0
        blocks = max(b for b in range(1, min(units, n_steps) + 1) if units % b == 0)
        brows = rows // blocks
        idx = functools.partial(lambda *g, blocks: (linear_step(*g) * blocks) // n_steps, blocks=blocks)
        in_specs.append(pl.BlockSpec((None, brows, cols),
                                     functools.partial(lambda *g, idx, layer: (layer, idx(*g), 0), idx=idx, layer=c.layer)))
        out_specs.append(pl.BlockSpec((brows, cols), functools.partial(lambda *g, idx: (idx(*g), 0), idx=idx)))
        out_shapes.append(jax.ShapeDtypeStruct((rows, cols), _BF16))
        args.append(c.stacked)
    return in_specs, out_specs, out_shapes, args


def _run_cvts(src_refs, dst_refs):
    for s, d in zip(src_refs, dst_refs):
        d[...] = s[...].astype(d.dtype)


def _relu2(y):
    r = jnp.maximum(y, 0.0)
    return r * r


def _mm_kernel(*refs, epilogue, has_row_ss, has_residual, has_norm, n_cvt):
    refs = list(refs)
    a_ref, w_ref = refs.pop(0), refs.pop(0)
    ss_in_ref = refs.pop(0) if has_row_ss else None
    r_ref = refs.pop(0) if has_residual else None
    gain_ref = refs.pop(0) if has_norm else None
    cvt_src = [refs.pop(0) for _ in range(n_cvt)]
    o_ref = refs.pop(0)
    xg_ref, ss_ref = (refs.pop(0), refs.pop(0)) if has_norm else (None, None)
    cvt_dst = refs
    y = _dot(a_ref[...], w_ref[...])
    if has_row_ss:
        y = _scale_rows_rms(y, ss_in_ref, a_ref.shape[1])
    if epilogue is not None:
        y = epilogue(y)
    if has_residual:
        y = r_ref[...] + y
    o_ref[...] = y.astype(o_ref.dtype)
    _run_cvts(cvt_src, cvt_dst)
    if has_norm:
        _emit_normed(y, pl.program_id(1) == 0, gain_ref, xg_ref, ss_ref)


def _matmul(a, w, *, n_out, col_block0=0, tm, tn, out_dtype, row_ss=None, epilogue=None, residual=None,
            norm_gain=None, cvts=(), name):
    m, k = a.shape
    tm, tn = min(tm, m), min(tn, n_out)
    grid = (m // tm, n_out // tn)
    n_steps = grid[0] * grid[1]
    in_specs = [pl.BlockSpec((tm, k), lambda i, j: (i, 0)),
                pl.BlockSpec((k, tn), lambda i, j: (0, j + col_block0))]
    args = [a, w]
    out_specs = [pl.BlockSpec((tm, tn), lambda i, j: (i, j))]
    out_shapes = [jax.ShapeDtypeStruct((m, n_out), out_dtype)]
    if row_ss is not None:
        in_specs.append(pl.BlockSpec((tm, V7X_LANES), lambda i, j: (i, 0)))
        args.append(row_ss)
    if residual is not None:
        in_specs.append(pl.BlockSpec((tm, tn), lambda i, j: (i, j)))
        args.append(residual)
    if norm_gain is not None:
        in_specs.append(pl.BlockSpec((1, tn), lambda i, j: (0, j)))
        args.append(norm_gain.reshape(1, n_out).astype(_F32))
        out_specs += [pl.BlockSpec((tm, tn), lambda i, j: (i, j)),
                      pl.BlockSpec((tm, V7X_LANES), lambda i, j: (i, 0))]
        out_shapes += [jax.ShapeDtypeStruct((m, n_out), _BF16), jax.ShapeDtypeStruct((m, V7X_LANES), _F32)]
    c_in, c_out, c_shapes, c_args = _cvt_plumbing(cvts, n_steps, lambda i, j: i * grid[1] + j)
    return pl.pallas_call(
        functools.partial(_mm_kernel, epilogue=epilogue, has_row_ss=row_ss is not None,
                          has_residual=residual is not None, has_norm=norm_gain is not None, n_cvt=len(cvts)),
        grid=grid,
        in_specs=in_specs + c_in,
        out_specs=out_specs + c_out,
        out_shape=out_shapes + c_shapes,
        compiler_params=_compiler_params(2),
        name=name,
    )(*args, *c_args)


def _mm_ktiled_res_kernel(*refs, has_norm, n_cvt, n_k):
    refs = list(refs)
    a_ref, w_ref, r_ref = refs.pop(0), refs.pop(0), refs.pop(0)
    gain_ref = refs.pop(0) if has_norm else None
    cvt_src = [refs.pop(0) for _ in range(n_cvt)]
    o_ref = refs.pop(0)
    xg_ref, ss_ref = (refs.pop(0), refs.pop(0)) if has_norm else (None, None)
    cvt_dst = refs
    k = pl.program_id(2)

    def step(first, last):
        y = (r_ref[...] if first else o_ref[...]) + _dot(a_ref[...], w_ref[...])
        o_ref[...] = y
        _run_cvts(cvt_src, cvt_dst)
        if last and has_norm:
            _emit_normed(y, pl.program_id(1) == 0, gain_ref, xg_ref, ss_ref)

    roles = [(True, n_k == 1, k == 0)]
    if n_k > 2 or (n_k == 2 and not has_norm):
        roles.append((False, False, (k > 0) if not has_norm else jnp.logical_and(k > 0, k < n_k - 1)))
    if n_k > 1 and has_norm:
        roles.append((False, True, k == n_k - 1))
    for first, last, cond in roles:
        pl.when(cond)(functools.partial(step, first, last))


def _matmul_ktiled_res(a, w, residual, *, tm, tn, tk, norm_gain=None, cvts=(), name):
    m, k = a.shape
    _, n = w.shape
    tm, tn, tk = min(tm, m), min(tn, n), min(tk, k)
    grid = (m // tm, n // tn, k // tk)
    n_steps = grid[0] * grid[1] * grid[2]
    in_specs = [pl.BlockSpec((tm, tk), lambda i, j, q: (i, q)),
                pl.BlockSpec((tk, tn), lambda i, j, q: (q, j)),
                pl.BlockSpec((tm, tn), lambda i, j, q: (i, j))]
    args = [a, w, residual]
    out_specs = [pl.BlockSpec((tm, tn), lambda i, j, q: (i, j))]
    out_shapes = [jax.ShapeDtypeStruct((m, n), _F32)]
    if norm_gain is not None:
        in_specs.append(pl.BlockSpec((1, tn), lambda i, j, q: (0, j)))
        args.append(norm_gain.reshape(1, n).astype(_F32))
        out_specs += [pl.BlockSpec((tm, tn), lambda i, j, q: (i, j)),
                      pl.BlockSpec((tm, V7X_LANES), lambda i, j, q: (i, 0))]
        out_shapes += [jax.ShapeDtypeStruct((m, n), _BF16), jax.ShapeDtypeStruct((m, V7X_LANES), _F32)]
    c_in, c_out, c_shapes, c_args = _cvt_plumbing(
        cvts, n_steps, lambda i, j, q: (i * grid[1] + j) * grid[2] + q)
    return pl.pallas_call(
        functools.partial(_mm_ktiled_res_kernel, has_norm=norm_gain is not None, n_cvt=len(cvts), n_k=grid[2]),
        grid=grid,
        in_specs=in_specs + c_in,
        out_specs=out_specs + c_out,
        out_shape=out_shapes + c_shapes,
        compiler_params=_compiler_params(3),
        name=name,
    )(*args, *c_args)


def _merge_kernel(*refs, n_cvt):
    h_ref, ss_ref, oa_ref, oh_ref, wga_ref, wgh_ref, wa_ref, wh_ref = refs[:8]
    cvt_src = refs[8:8 + n_cvt]
    o_ref = refs[8 + n_cvt]
    cvt_dst = refs[9 + n_cvt:]
    h = h_ref[...]
    d = h_ref.shape[1]
    ga = _scale_rows_rms(_dot(h, wga_ref[...]), ss_ref, d)
    gh = _scale_rows_rms(_dot(h, wgh_ref[...]), ss_ref, d)
    ba = _dot(oa_ref[...], wa_ref[...])
    bh = _dot(oh_ref[...], wh_ref[...])
    o_ref[...] = (jax.nn.sigmoid(ga) * ba + jax.nn.sigmoid(gh) * bh).astype(o_ref.dtype)
    _run_cvts(cvt_src, cvt_dst)


def _merge(nx, o_attn, o_hgrn, w_in_b, w_branch_b, *, gate_col0, tm, tn, cvts=()):
    m, d = nx.xg.shape
    tm, tn = min(tm, m), min(tn, d)
    n_tiles = d // tn
    assert ATTN_Q_DIM % HG_DIM == 0 and gate_col0 % tn == 0
    g0 = gate_col0 // tn
    grid = (m // tm, n_tiles)
    c_in, c_out, c_shapes, c_args = _cvt_plumbing(cvts, grid[0] * grid[1], lambda i, j: i * n_tiles + j)
    return pl.pallas_call(
        functools.partial(_merge_kernel, n_cvt=len(cvts)),
        grid=grid,
        in_specs=[
            pl.BlockSpec((tm, d), lambda i, j: (i, 0)),
            pl.BlockSpec((tm, V7X_LANES), lambda i, j: (i, 0)),
            pl.BlockSpec((tm, ATTN_Q_DIM), lambda i, j: (i, 0)),
            pl.BlockSpec((tm, HG_DIM), lambda i, j: (i, 0)),
            pl.BlockSpec((d, tn), lambda i, j: (0, g0 + j)),
            pl.BlockSpec((d, tn), lambda i, j: (0, g0 + n_tiles + j)),
            pl.BlockSpec((ATTN_Q_DIM, tn), lambda i, j: (0, j)),
            pl.BlockSpec((HG_DIM, tn), lambda i, j: (ATTN_Q_DIM // HG_DIM, j)),
        ] + c_in,
        out_specs=[pl.BlockSpec((tm, tn), lambda i, j: (i, j))] + c_out,
        out_shape=[jax.ShapeDtypeStruct((m, d), _BF16)] + c_shapes,
        compiler_params=_compiler_params(2),
        name="gated_merge",
    )(nx.xg, nx.ss, o_attn, o_hgrn, w_in_b, w_in_b, w_branch_b, w_branch_b, *c_args)


def _bucket_table():
    ki = np.arange(2 * ATTN_BLOCK)[:, None]
    qi = np.arange(ATTN_BLOCK)[None, :]
    dist = qi + ATTN_BLOCK - ki
    in_window = (dist >= 0) & (dist < ATTN_BLOCK)
    in_window_first = in_window & (ki >= ATTN_BLOCK)
    dist = np.maximum(dist, 0)
    max_exact = NUM_BUCKETS // 2
    d = np.maximum(dist, 1).astype(np.float64)
    val = np.log(d / max_exact) / math.log(MAX_DISTANCE / max_exact) * (NUM_BUCKETS - max_exact)
    large = max_exact + np.trunc(val).astype(np.int64)
    frac = np.abs(val - np.round(val))[in_window & (dist > max_exact)]
    assert frac.min() > 1e-3
    bucket = np.where(dist < max_exact, dist, np.minimum(large, NUM_BUCKETS - 1))
    return bucket.reshape(-1), in_window_first.reshape(-1), in_window.reshape(-1)


def _bias_kernel(rbt_ref, onehot_ref, mask_ref, o_ref):
    a1, a2, a3 = _split3_bf16(rbt_ref[...])
    lhs = jnp.concatenate([a1, a2, a3], axis=1)
    base = _dot(lhs, onehot_ref[...])
    o_ref[0] = base + mask_ref[0]
    o_ref[1] = base + mask_ref[1]


def _bias_band(rel_bias, *, tn=4096):
    bucket, first, other = _bucket_table()
    n = bucket.shape[0]
    onehot = (np.arange(NUM_BUCKETS)[:, None] == bucket[None, :]).astype(np.float32)
    onehot3 = jnp.asarray(np.concatenate([onehot] * 3, axis=0), dtype=_BF16)
    maskadd = jnp.asarray(np.where(np.stack([first, other])[:, None, :], 0.0, -np.inf), dtype=_F32)
    out = pl.pallas_call(
        _bias_kernel,
        grid=(n // tn,),
        in_specs=[pl.BlockSpec((ATTN_HEADS, NUM_BUCKETS), lambda j: (0, 0)),
                  pl.BlockSpec((3 * NUM_BUCKETS, tn), lambda j: (0, j)),
                  pl.BlockSpec((2, 1, tn), lambda j: (0, 0, j))],
        out_specs=pl.BlockSpec((2, ATTN_HEADS, tn), lambda j: (0, 0, j)),
        out_shape=jax.ShapeDtypeStruct((2, ATTN_HEADS, n), _F32),
        compiler_params=_compiler_params(1),
        name="rel_bias_band",
    )(rel_bias.astype(_F32).T, onehot3, maskadd)
    return out.reshape(2, ATTN_HEADS, 2 * ATTN_BLOCK, ATTN_BLOCK)


def _group_rms(x, blockdiag, gain):
    x2 = x * x
    hi = x2.astype(_BF16)
    lo = (x2 - hi.astype(_F32)).astype(_BF16)
    ss = _dot(hi, blockdiag) + _dot(lo, blockdiag)
    return x * lax.rsqrt(ss * (1.0 / ATTN_HEAD_DIM) + EPS) * gain


def _pair_stack(slab, low_half):
    swapped = pltpu.roll(slab, ATTN_HEAD_DIM, 1)
    zero = jnp.zeros_like(slab)
    even = jnp.concatenate([jnp.where(low_half, slab, zero), jnp.where(low_half, zero, swapped)], axis=0)
    odd = jnp.concatenate([jnp.where(low_half, swapped, zero), jnp.where(low_half, zero, slab)], axis=0)
    return even.astype(_BF16), odd.astype(_BF16)


class _AttnSteps(NamedTuple):
    prologue: Callable[[], None]
    scores: Callable[[int], None]
    finish: Callable[[int], None]
    n_pairs: int


def _attn_steps(sink_ref, q_ref, kvc_ref, kvp_ref, bias_ref, gq_ref, gk_ref, bd_ref, o_ref):
    w = 2 * ATTN_BLOCK
    st = {}

    def prologue():
        bd = bd_ref[...]
        kv = jnp.concatenate([kvp_ref[...], kvc_ref[...]], axis=0)
        kn = _group_rms(kv[:, :ATTN_KV_DIM], bd, gk_ref[...])
        v = kv[:, ATTN_KV_DIM:]
        lane = lax.broadcasted_iota(jnp.int32, (w, V7X_LANES), 1)
        low_half = lane < ATTN_HEAD_DIM
        k_stacks, v_stacks = [], []
        for s in range(ATTN_KV_DIM // V7X_LANES):
            k_stacks.extend(_pair_stack(kn[:, s * V7X_LANES:(s + 1) * V7X_LANES], low_half))
            v_stacks.extend(_pair_stack(v[:, s * V7X_LANES:(s + 1) * V7X_LANES], low_half))
        st["k"], st["v"], st["bd"] = k_stacks, v_stacks, bd

    def kv_head(pair):
        return (pair * V7X_LANES) // (ATTN_GROUP * ATTN_HEAD_DIM)

    def scores(pair):
        s, half = divmod(pair, 2)
        if half == 0:
            q = q_ref[:, s * w:(s + 1) * w]
            st["qn"] = (_group_rms(q, st["bd"], gq_ref[...]) * ATTN_SCALE).astype(_BF16)
        qn = st["qn"][:, half * V7X_LANES:(half + 1) * V7X_LANES]
        st["sc", pair] = _nt_dot(st["k"][kv_head(pair)], qn)

    def finish(pair):
        sc = st.pop(("sc", pair))
        probs = []
        for e in range(2):
            head = 2 * pair + e
            se = sc[e * w:(e + 1) * w] + bias_ref[head]
            sink = sink_ref[head]
            mx = jnp.maximum(jnp.max(se, axis=0, keepdims=True), sink)
            p = jnp.exp(se - mx)
            denom = jnp.sum(p, axis=0, keepdims=True) + jnp.exp(sink - mx)
            probs.append((p * (1.0 / denom)).astype(_BF16))
        o = _tn_dot(jnp.concatenate(probs, axis=0), st["v"][kv_head(pair)])
        o_ref[:, pair * V7X_LANES:(pair + 1) * V7X_LANES] = o.astype(o_ref.dtype)

    return _AttnSteps(prologue, scores, finish, ATTN_HEADS // 2)


def _proj_attn_kernel(a_ref, w_ref, ss_ref, sink_ref, q_ref, kvc_ref, kvp_ref, bias_ref, gq_ref, gk_ref, bd_ref,
                      hg_ref, o_ref):
    attn = _attn_steps(sink_ref, q_ref, kvc_ref, kvp_ref, bias_ref, gq_ref, gk_ref, bd_ref, o_ref)
    kc = a_ref.shape[1] // attn.n_pairs
    attn.prologue()
    attn.scores(0)
    for p in range(attn.n_pairs):
        part = _dot(a_ref[:, p * kc:(p + 1) * kc], w_ref[p * kc:(p + 1) * kc, :])
        if p + 1 < attn.n_pairs:
            attn.scores(p + 1)
        if p == 0:
            hg_ref[...] = part
        elif p + 1 < attn.n_pairs:
            hg_ref[...] += part
        else:
            hg_ref[...] = _scale_rows_rms(hg_ref[...] + part, ss_ref, a_ref.shape[1])
        attn.finish(p)


def _proj_hgrn_with_attention(nx, w_in_b, col_block0, qkv, sinks, bias, q_gain, k_gain, batch, seq, *, tm, tn):
    m, k = nx.xg.shape
    tm = min(tm, m)
    nb = seq // ATTN_BLOCK
    grid = (m // tm, HGRN_IN_DIM // tn)
    nj = grid[1]
    assert grid[0] * nj == batch * nb and k % (ATTN_HEADS // 2) == 0
    kv_col = ATTN_Q_DIM // (2 * ATTN_KV_DIM)
    assert ATTN_Q_DIM % (2 * ATTN_KV_DIM) == 0
    blockdiag = jnp.asarray(np.kron(np.eye(2 * ATTN_BLOCK // ATTN_HEAD_DIM), np.ones((ATTN_HEAD_DIM,) * 2)), _BF16)
    tile = lambda g: jnp.tile(g.reshape(1, ATTN_HEAD_DIM).astype(_F32), (1, 2 * ATTN_BLOCK // ATTN_HEAD_DIM))
    blk = lambda i, j: i * nj + j
    first = lambda i, j: (blk(i, j) % nb) == 0
    return pl.pallas_call(
        _proj_attn_kernel,
        grid=grid,
        in_specs=[
            pl.BlockSpec((tm, k), lambda i, j: (i, 0)),
            pl.BlockSpec((k, tn), lambda i, j: (0, j + col_block0)),
            pl.BlockSpec((tm, V7X_LANES), lambda i, j: (i, 0)),
            pl.BlockSpec(memory_space=pltpu.SMEM),
            pl.BlockSpec((ATTN_BLOCK, ATTN_Q_DIM), lambda i, j: (blk(i, j), 0)),
            pl.BlockSpec((ATTN_BLOCK, 2 * ATTN_KV_DIM), lambda i, j: (blk(i, j), kv_col)),
            pl.BlockSpec((ATTN_BLOCK, 2 * ATTN_KV_DIM),
                         lambda i, j: (jnp.where(first(i, j), blk(i, j), blk(i, j) - 1), kv_col)),
            pl.BlockSpec((None, ATTN_HEADS, 2 * ATTN_BLOCK, ATTN_BLOCK),
                         lambda i, j: (jnp.where(first(i, j), 0, 1), 0, 0, 0)),
            pl.BlockSpec((1, 2 * ATTN_BLOCK), lambda i, j: (0, 0)),
            pl.BlockSpec((1, 2 * ATTN_BLOCK), lambda i, j: (0, 0)),
            pl.BlockSpec((2 * ATTN_BLOCK, 2 * ATTN_BLOCK), lambda i, j: (0, 0)),
        ],
        out_specs=[pl.BlockSpec((tm, tn), lambda i, j: (i, j)),
                   pl.BlockSpec((ATTN_BLOCK, ATTN_Q_DIM), lambda i, j: (blk(i, j), 0))],
        out_shape=[jax.ShapeDtypeStruct((m, HGRN_IN_DIM), _F32),
                   jax.ShapeDtypeStruct((m, ATTN_Q_DIM), _BF16)],
        compiler_params=_compiler_params(2),
        name="proj_hgrn_attn",
    )(nx.xg, w_in_b, nx.ss, sinks.astype(_F32), qkv, qkv, qkv, bias, tile(q_gain), tile(k_gain), blockdiag)


_HG_LEVELS = tuple(HG_CHUNK >> (i + 1) for i in range(int(math.log2(HG_CHUNK))))
_HG_PIPELINE_LEAD = 3
_HG_MATMUL_LEVELS = tuple(m for m in _HG_LEVELS if m < V7X_F32_SUBLANES)


def _hgrn_tables():
    c = HG_CHUNK
    t = np.arange(c)
    blocks = [(t[None, :] <= t[:, None])]
    masks = [np.eye(c, dtype=bool)]
    for m in _HG_LEVELS:
        ref = (t // (2 * m)) * (2 * m) + m - 1
        upper = (t % (2 * m)) >= m
        if m in _HG_MATMUL_LEVELS:
            up_rows = upper[:, None] & (t[None, :] > ref[:, None]) & (t[None, :] <= t[:, None])
            lo_rows = (~upper)[:, None] & (t[None, :] > t[:, None]) & (t[None, :] <= ref[:, None])
            blocks.append(up_rows | lo_rows)
        same = (t[:, None] // (2 * m)) == (t[None, :] // (2 * m))
        masks.append(same & upper[:, None] & (~upper)[None, :])
    sums = np.concatenate(blocks, axis=0).astype(np.float32)
    return np.concatenate([sums] * 3, axis=1), np.stack(masks).astype(np.float32)


def _hgrn_kernel(p_ref, lbl_ref, gain_ref, sums_ref, masks_ref, o_ref, state_ref, *, layer, rows):
    c = HG_CHUNK

    @pl.when(pl.program_id(1) == 0)
    def _():
        state_ref[...] = jnp.zeros_like(state_ref)

    lg = lbl_ref[...]
    e = jnp.exp(lg - jnp.max(lg, axis=0, keepdims=True))
    sm = e / jnp.sum(e, axis=0, keepdims=True)
    lb_all = jnp.zeros_like(sm[0:1])
    for i in range(1, layer + 1):
        lb_all = lb_all + sm[i:i + 1]

    row = lax.broadcasted_iota(jnp.int32, (c, 1), 0)
    gain = gain_ref[...]
    sums = sums_ref[...]

    def chunk_body(ci, carry):
        r0 = pl.multiple_of(ci * c, c)
        st = [dict() for _ in range(HG_HEADS)]

        def seg(h, which):
            return p_ref[pl.ds(r0, c), which * HG_DIM + h * HG_DK:which * HG_DIM + (h + 1) * HG_DK]

        def gate_math(h):
            lb = lb_all[:, h * HG_DK:(h + 1) * HG_DK]
            hq = seg(h, 0)
            forget = lb + (1.0 - lb) * jax.nn.sigmoid(seg(h, 1))
            st[h]["kk"] = 1.0 - forget
            st[h]["qf"] = hq * jax.nn.sigmoid(hq)
            return jnp.concatenate(_split3_bf16(jnp.log(forget)), axis=0)

        def gates(h):
            if h % 2 == 0:
                both = _dot(sums, jnp.concatenate([gate_math(h), gate_math(h + 1)], axis=1))
                st[h]["expo"], st[h + 1]["expo"] = both[:, :HG_DK], both[:, HG_DK:]

        def level_decay(expo, m):
            if m in _HG_MATMUL_LEVELS:
                k = 1 + _HG_MATMUL_LEVELS.index(m)
                return jnp.exp(expo[k * c:(k + 1) * c])
            bcum = expo[0:c]
            blocks = []
            for s0 in range(0, c, 2 * m):
                blocks.append(-jnp.abs(bcum[s0:s0 + 2 * m] - bcum[s0 + m - 1:s0 + m]))
            return jnp.exp(jnp.concatenate(blocks, axis=0))

        def products(h):
            d = st[h]
            qf, kk = d.pop("qf"), d.pop("kk")
            d["v"] = seg(h, 2).astype(_BF16)
            expo = d.pop("expo")
            bcum = expo[0:c]
            e_cum = jnp.exp(bcum)
            e_end = jnp.exp(bcum[c - 1:c] - bcum)
            state_t = state_ref[h]
            d["o_inter"] = _nt_dot((qf * e_cum).astype(_BF16), state_t.astype(_BF16))
            parts = [_nt_dot(qf.astype(_BF16), kk.astype(_BF16))]
            for m in _HG_LEVELS:
                upper = (row & m) != 0
                z = (jnp.where(upper, qf, kk) * level_decay(expo, m)).astype(_BF16)
                parts.append(_nt_dot(z, z))
            d["parts"] = parts
            state_ref[h] = state_t * e_cum[c - 1:c] + _tn_dot(d["v"], (kk * e_end).astype(_BF16))

        def output(h):
            d = st[h]
            parts = d.pop("parts")
            scores = parts[0] * masks_ref[0]
            for li in range(len(_HG_LEVELS)):
                scores = scores + parts[1 + li] * masks_ref[1 + li]
            o = d.pop("o_inter") + _dot(scores.astype(_BF16), d.pop("v"))
            hg = seg(h, 3)
            y = o * lax.rsqrt(jnp.mean(o * o, axis=-1, keepdims=True) + EPS) * gain
            y = y * (hg * jax.nn.sigmoid(hg))
            o_ref[pl.ds(r0, c), h * HG_DV:(h + 1) * HG_DV] = y.astype(o_ref.dtype)

        lead = _HG_PIPELINE_LEAD
        for h in range(-2 * lead, HG_HEADS):
            if 0 <= h + 2 * lead < HG_HEADS:
                gates(h + 2 * lead)
            if 0 <= h + lead < HG_HEADS:
                products(h + lead)
            if 0 <= h:
                output(h)
        return carry

    lax.fori_loop(0, rows // c, chunk_body, 0)


def _hgrn(p, lb_logits, norm_gain, layer, batch, seq, *, rows=256):
    rows = min(rows, seq)
    steps = seq // rows
    depth = lb_logits.shape[0]
    sums, masks = _hgrn_tables()
    return pl.pallas_call(
        functools.partial(_hgrn_kernel, layer=layer, rows=rows),
        grid=(batch, steps),
        in_specs=[
            pl.BlockSpec((rows, HGRN_IN_DIM), lambda b, s: (b * steps + s, 0)),
            pl.BlockSpec((depth, HG_DIM), lambda b, s: (0, 0)),
            pl.BlockSpec((1, HG_DV), lambda b, s: (0, 0)),
            pl.BlockSpec(sums.shape, lambda b, s: (0, 0)),
            pl.BlockSpec(masks.shape, lambda b, s: (0, 0, 0)),
        ],
        out_specs=pl.BlockSpec((rows, HG_DIM), lambda b, s: (b * steps + s, 0)),
        out_shape=jax.ShapeDtypeStruct((batch * seq, HG_DIM), _BF16),
        scratch_shapes=[pltpu.VMEM((HG_HEADS, HG_DV, HG_DK), _F32)],
        compiler_params=_compiler_params(2),
        name="hgrn2_scan",
    )(p, lb_logits.astype(_F32), norm_gain.reshape(1, HG_DV).astype(_F32),
      jnp.asarray(sums, _BF16), jnp.asarray(masks, _F32))


def kernel(x, attn_norm_gain, w_in, q_norm_gain, k_norm_gain, attn_sinks, rel_bias,
           hgrn_lb_logits, hgrn_norm_gain, w_branch, w_out, mlp_norm_gain, w_up, w_down):
    batch, seq, d = x.shape
    depth = w_in.shape[0]
    m = batch * seq
    xf = x.reshape(m, d).astype(_F32)
    bias = _bias_band(rel_bias)
    t = _TILES
    hg_col0 = QKV_DIM // t.proj_tn
    gate_col0 = QKV_DIM + HGRN_IN_DIM
    assert QKV_DIM % t.proj_tn == 0

    w_in_b = w_in[0].astype(_BF16)
    w_br_b = w_branch[0].astype(_BF16)
    w_o_b = w_out[0].astype(_BF16)
    nx = _normed_input(xf, attn_norm_gain[0])
    for l in range(depth):
        qkv, = _matmul(nx.xg, w_in_b, n_out=QKV_DIM, tm=t.tm, tn=t.qkv_tn, out_dtype=_F32, row_ss=nx.ss,
                       name="proj_qkv")
        hgp, o_attn = _proj_hgrn_with_attention(nx, w_in_b, hg_col0, qkv, attn_sinks[l], bias, q_norm_gain[l],
                                                k_norm_gain[l], batch, seq, tm=t.tm, tn=t.proj_tn)
        o_hgrn = _hgrn(hgp, hgrn_lb_logits, hgrn_norm_gain[l], l, batch, seq)
        merged, w_u_b = _merge(nx, o_attn, o_hgrn, w_in_b, w_br_b, gate_col0=gate_col0,
                               tm=t.tm, tn=t.merge_tn, cvts=(_Cvt(w_up, l),))
        xf, xg, ss = _matmul(merged, w_o_b, n_out=d, tm=t.tm, tn=t.proj_tn, out_dtype=_F32, residual=xf,
                             norm_gain=mlp_norm_gain[l], name="out_proj")

        u, w_d_b = _matmul(xg, w_u_b, n_out=w_up.shape[2], tm=t.tm, tn=t.up_tn, out_dtype=_BF16,
                           row_ss=ss, epilogue=_relu2, cvts=(_Cvt(w_down, l),), name="mlp_up")
        if l + 1 == depth:
            xf, = _matmul_ktiled_res(u, w_d_b, xf, tm=t.tm, tn=t.down_tn, tk=t.down_tk, name="mlp_down")
        else:
            xf, xg, ss, w_in_b, w_br_b, w_o_b = _matmul_ktiled_res(
                u, w_d_b, xf, tm=t.tm, tn=t.down_tn, tk=t.down_tk, norm_gain=attn_norm_gain[l + 1],
                cvts=(_Cvt(w_in, l + 1), _Cvt(w_branch, l + 1), _Cvt(w_out, l + 1)), name="mlp_down")
            nx = _Normed(xg, ss)
    return xf.reshape(batch, seq, d).astype(x.dtype)
```

```python
import functools
import math
from typing import Callable, NamedTuple

import numpy as np
import jax
import jax.numpy as jnp
from jax import lax
from jax.experimental import pallas as pl
from jax.experimental.pallas import tpu as pltpu

ATTN_HEADS = 32
ATTN_KV_HEADS = 4
ATTN_HEAD_DIM = 64
ATTN_GROUP = ATTN_HEADS // ATTN_KV_HEADS
ATTN_BLOCK = 128
ATTN_SCALE = ATTN_HEAD_DIM ** -0.5
NUM_BUCKETS = 32
MAX_DISTANCE = 128
HG_HEADS = 8
HG_DK = 128
HG_DV = 128
HG_CHUNK = 64
EPS = 1e-6

ATTN_Q_DIM = ATTN_HEADS * ATTN_HEAD_DIM
ATTN_KV_DIM = ATTN_KV_HEADS * ATTN_HEAD_DIM
QKV_DIM = ATTN_Q_DIM + 2 * ATTN_KV_DIM
HG_DIM = HG_HEADS * HG_DK
HGRN_IN_DIM = 4 * HG_DIM

V7X_LANES = 128
V7X_F32_SUBLANES = 8
BF16_SUBLANES = 16
V7X_VMEM_LIMIT_BYTES = 56 * 1024 * 1024

_BF16 = jnp.bfloat16
_F32 = jnp.float32


class _Tiles(NamedTuple):
    tm: int = 1024
    qkv_tn: int = 1280
    proj_tn: int = 512
    merge_tn: int = 256
    up_tn: int = 1024
    down_tn: int = 1024
    down_tk: int = 2048


_TILES = _Tiles()


def _nt_dot(a, b):
    return lax.dot_general(a, b, (((1,), (1,)), ((), ())), preferred_element_type=_F32)


def _tn_dot(a, b):
    return lax.dot_general(a, b, (((0,), (0,)), ((), ())), preferred_element_type=_F32)


def _dot(a, b):
    return jnp.dot(a, b, preferred_element_type=_F32)


def _split3_bf16(x):
    a1 = x.astype(_BF16)
    r1 = x - a1.astype(_F32)
    a2 = r1.astype(_BF16)
    a3 = (r1 - a2.astype(_F32)).astype(_BF16)
    return a1, a2, a3


def _compiler_params(n_grid):
    return pltpu.CompilerParams(
        dimension_semantics=("arbitrary",) * n_grid,
        vmem_limit_bytes=V7X_VMEM_LIMIT_BYTES,
    )


class _Normed(NamedTuple):
    xg: jax.Array
    ss: jax.Array


def _emit_normed(y, first_col_tile, gain_ref, xg_ref, ss_ref):
    xg_ref[...] = (y * gain_ref[...]).astype(xg_ref.dtype)
    part = jnp.broadcast_to(jnp.sum(y * y, axis=1, keepdims=True), ss_ref.shape)

    @pl.when(first_col_tile)
    def _():
        ss_ref[...] = part

    @pl.when(jnp.logical_not(first_col_tile))
    def _():
        ss_ref[...] += part


def _scale_rows_rms(y, ss_ref, d):
    r = lax.rsqrt(ss_ref[...] * (1.0 / d) + EPS)
    return jnp.concatenate([y[:, c * V7X_LANES:(c + 1) * V7X_LANES] * r for c in range(y.shape[1] // V7X_LANES)],
                           axis=1)


def _normed_input_kernel(x_ref, g_ref, xg_ref, ss_ref):
    _emit_normed(x_ref[...], True, g_ref, xg_ref, ss_ref)


def _normed_input(x, gain, *, rows=256):
    m, d = x.shape
    rows = min(rows, m)
    xg, ss = pl.pallas_call(
        _normed_input_kernel,
        grid=(m // rows,),
        in_specs=[pl.BlockSpec((rows, d), lambda i: (i, 0)),
                  pl.BlockSpec((1, d), lambda i: (0, 0))],
        out_specs=[pl.BlockSpec((rows, d), lambda i: (i, 0)),
                   pl.BlockSpec((rows, V7X_LANES), lambda i: (i, 0))],
        out_shape=[jax.ShapeDtypeStruct((m, d), _BF16), jax.ShapeDtypeStruct((m, V7X_LANES), _F32)],
        compiler_params=_compiler_params(1),
        name="normed_input",
    )(x, gain.reshape(1, d).astype(_F32))
    return _Normed(xg, ss)


class _Cvt(NamedTuple):
    stacked: jax.Array
    layer: int


def _cvt_plumbing(cvts, n_steps, linear_step):
    in_specs, out_specs, out_shapes, args = [], [], [], []
    for c in cvts:
        _, rows, cols = c.stacked.shape
        units = rows // BF16_SUBLANES
        assert rows % BF16_SUBLANES == 0
        blocks = max(b for b in range(1, min(units, n_steps) + 1) if units % b == 0)
        brows = rows // blocks
        idx = functools.partial(lambda *g, blocks: (linear_step(*g) * blocks) // n_steps, blocks=blocks)
        in_specs.append(pl.BlockSpec((None, brows, cols),
                                     functools.partial(lambda *g, idx, layer: (layer, idx(*g), 0), idx=idx, layer=c.layer)))
        out_specs.append(pl.BlockSpec((brows, cols), functools.partial(lambda *g, idx: (idx(*g), 0), idx=idx)))
        out_shapes.append(jax.ShapeDtypeStruct((rows, cols), _BF16))
        args.append(c.stacked)
    return in_specs, out_specs, out_shapes, args


def _run_cvts(src_refs, dst_refs):
    for s, d in zip(src_refs, dst_refs):
        d[...] = s[...].astype(d.dtype)


def _relu2(y):
    r = jnp.maximum(y, 0.0)
    return r * r


def _mm_kernel(*refs, epilogue, has_row_ss, has_residual, has_norm, n_cvt):
    refs = list(refs)
    a_ref, w_ref = refs.pop(0), refs.pop(0)
    ss_in_ref = refs.pop(0) if has_row_ss else None
    r_ref = refs.pop(0) if has_residual else None
    gain_ref = refs.pop(0) if has_norm else None
    cvt_src = [refs.pop(0) for _ in range(n_cvt)]
    o_ref = refs.pop(0)
    xg_ref, ss_ref = (refs.pop(0), refs.pop(0)) if has_norm else (None, None)
    cvt_dst = refs
    y = _dot(a_ref[...], w_ref[...])
    if has_row_ss:
        y = _scale_rows_rms(y, ss_in_ref, a_ref.shape[1])
    if epilogue is not None:
        y = epilogue(y)
    if has_residual:
        y = r_ref[...] + y
    o_ref[...] = y.astype(o_ref.dtype)
    _run_cvts(cvt_src, cvt_dst)
    if has_norm:
        _emit_normed(y, pl.program_id(1) == 0, gain_ref, xg_ref, ss_ref)


def _matmul(a, w, *, n_out, col_block0=0, tm, tn, out_dtype, row_ss=None, epilogue=None, residual=None,
            norm_gain=None, cvts=(), name):
    m, k = a.shape
    tm, tn = min(tm, m), min(tn, n_out)
    grid = (m // tm, n_out // tn)
    n_steps = grid[0] * grid[1]
    in_specs = [pl.BlockSpec((tm, k), lambda i, j: (i, 0)),
                pl.BlockSpec((k, tn), lambda i, j: (0, j + col_block0))]
    args = [a, w]
    out_specs = [pl.BlockSpec((tm, tn), lambda i, j: (i, j))]
    out_shapes = [jax.ShapeDtypeStruct((m, n_out), out_dtype)]
    if row_ss is not None:
        in_specs.append(pl.BlockSpec((tm, V7X_LANES), lambda i, j: (i, 0)))
        args.append(row_ss)
    if residual is not None:
        in_specs.append(pl.BlockSpec((tm, tn), lambda i, j: (i, j)))
        args.append(residual)
    if norm_gain is not None:
        in_specs.append(pl.BlockSpec((1, tn), lambda i, j: (0, j)))
        args.append(norm_gain.reshape(1, n_out).astype(_F32))
        out_specs += [pl.BlockSpec((tm, tn), lambda i, j: (i, j)),
                      pl.BlockSpec((tm, V7X_LANES), lambda i, j: (i, 0))]
        out_shapes += [jax.ShapeDtypeStruct((m, n_out), _BF16), jax.ShapeDtypeStruct((m, V7X_LANES), _F32)]
    c_in, c_out, c_shapes, c_args = _cvt_plumbing(cvts, n_steps, lambda i, j: i * grid[1] + j)
    return pl.pallas_call(
        functools.partial(_mm_kernel, epilogue=epilogue, has_row_ss=row_ss is not None,
                          has_residual=residual is not None, has_norm=norm_gain is not None, n_cvt=len(cvts)),
        grid=grid,
        in_specs=in_specs + c_in,
        out_specs=out_specs + c_out,
        out_shape=out_shapes + c_shapes,
        compiler_params=_compiler_params(2),
        name=name,
    )(*args, *c_args)


def _mm_ktiled_res_kernel(*refs, has_norm, n_cvt, n_k):
    refs = list(refs)
    a_ref, w_ref, r_ref = refs.pop(0), refs.pop(0), refs.pop(0)
    gain_ref = refs.pop(0) if has_norm else None
    cvt_src = [refs.pop(0) for _ in range(n_cvt)]
    o_ref = refs.pop(0)
    xg_ref, ss_ref = (refs.pop(0), refs.pop(0)) if has_norm else (None, None)
    cvt_dst = refs
    k = pl.program_id(2)

    def step(first, last):
        y = (r_ref[...] if first else o_ref[...]) + _dot(a_ref[...], w_ref[...])
        o_ref[...] = y
        _run_cvts(cvt_src, cvt_dst)
        if last and has_norm:
            _emit_normed(y, pl.program_id(1) == 0, gain_ref, xg_ref, ss_ref)

    roles = [(True, n_k == 1, k == 0)]
    if n_k > 2 or (n_k == 2 and not has_norm):
        roles.append((False, False, (k > 0) if not has_norm else jnp.logical_and(k > 0, k < n_k - 1)))
    if n_k > 1 and has_norm:
        roles.append((False, True, k == n_k - 1))
    for first, last, cond in roles:
        pl.when(cond)(functools.partial(step, first, last))


def _matmul_ktiled_res(a, w, residual, *, tm, tn, tk, norm_gain=None, cvts=(), name):
    m, k = a.shape
    _, n = w.shape
    tm, tn, tk = min(tm, m), min(tn, n), min(tk, k)
    grid = (m // tm, n // tn, k // tk)
    n_steps = grid[0] * grid[1] * grid[2]
    in_specs = [pl.BlockSpec((tm, tk), lambda i, j, q: (i, q)),
                pl.BlockSpec((tk, tn), lambda i, j, q: (q, j)),
                pl.BlockSpec((tm, tn), lambda i, j, q: (i, j))]
    args = [a, w, residual]
    out_specs = [pl.BlockSpec((tm, tn), lambda i, j, q: (i, j))]
    out_shapes = [jax.ShapeDtypeStruct((m, n), _F32)]
    if norm_gain is not None:
        in_specs.append(pl.BlockSpec((1, tn), lambda i, j, q: (0, j)))
        args.append(norm_gain.reshape(1, n).astype(_F32))
        out_specs += [pl.BlockSpec((tm, tn), lambda i, j, q: (i, j)),
                      pl.BlockSpec((tm, V7X_LANES), lambda i, j, q: (i, 0))]
        out_shapes += [jax.ShapeDtypeStruct((m, n), _BF16), jax.ShapeDtypeStruct((m, V7X_LANES), _F32)]
    c_in, c_out, c_shapes, c_args = _cvt_plumbing(
        cvts, n_steps, lambda i, j, q: (i * grid[1] + j) * grid[2] + q)
    return pl.pallas_call(
        functools.partial(_mm_ktiled_res_kernel, has_norm=norm_gain is not None, n_cvt=len(cvts), n_k=grid[2]),
        grid=grid,
        in_specs=in_specs + c_in,
        out_specs=out_specs + c_out,
        out_shape=out_shapes + c_shapes,
        compiler_params=_compiler_params(3),
        name=name,
    )(*args, *c_args)


def _merge_kernel(*refs, n_cvt):
    h_ref, ss_ref, oa_ref, oh_ref, wga_ref, wgh_ref, wa_ref, wh_ref = refs[:8]
    cvt_src = refs[8:8 + n_cvt]
    o_ref = refs[8 + n_cvt]
    cvt_dst = refs[9 + n_cvt:]
    h = h_ref[...]
    d = h_ref.shape[1]
    ga = _scale_rows_rms(_dot(h, wga_ref[...]), ss_ref, d)
    gh = _scale_rows_rms(_dot(h, wgh_ref[...]), ss_ref, d)
    ba = _dot(oa_ref[...], wa_ref[...])
    bh = _dot(oh_ref[...], wh_ref[...])
    o_ref[...] = (jax.nn.sigmoid(ga) * ba + jax.nn.sigmoid(gh) * bh).astype(o_ref.dtype)
    _run_cvts(cvt_src, cvt_dst)


def _merge(nx, o_attn, o_hgrn, w_in_b, w_branch_b, *, gate_col0, tm, tn, cvts=()):
    m, d = nx.xg.shape
    tm, tn = min(tm, m), min(tn, d)
    n_tiles = d // tn
    assert ATTN_Q_DIM % HG_DIM == 0 and gate_col0 % tn == 0
    g0 = gate_col0 // tn
    grid = (m // tm, n_tiles)
    c_in, c_out, c_shapes, c_args = _cvt_plumbing(cvts, grid[0] * grid[1], lambda i, j: i * n_tiles + j)
    return pl.pallas_call(
        functools.partial(_merge_kernel, n_cvt=len(cvts)),
        grid=grid,
        in_specs=[
            pl.BlockSpec((tm, d), lambda i, j: (i, 0)),
            pl.BlockSpec((tm, V7X_LANES), lambda i, j: (i, 0)),
            pl.BlockSpec((tm, ATTN_Q_DIM), lambda i, j: (i, 0)),
            pl.BlockSpec((tm, HG_DIM), lambda i, j: (i, 0)),
            pl.BlockSpec((d, tn), lambda i, j: (0, g0 + j)),
            pl.BlockSpec((d, tn), lambda i, j: (0, g0 + n_tiles + j)),
            pl.BlockSpec((ATTN_Q_DIM, tn), lambda i, j: (0, j)),
            pl.BlockSpec((HG_DIM, tn), lambda i, j: (ATTN_Q_DIM // HG_DIM, j)),
        ] + c_in,
        out_specs=[pl.BlockSpec((tm, tn), lambda i, j: (i, j))] + c_out,
        out_shape=[jax.ShapeDtypeStruct((m, d), _BF16)] + c_shapes,
        compiler_params=_compiler_params(2),
        name="gated_merge",
    )(nx.xg, nx.ss, o_attn, o_hgrn, w_in_b, w_in_b, w_branch_b, w_branch_b, *c_args)


def _bucket_table():
    ki = np.arange(2 * ATTN_BLOCK)[:, None]
    qi = np.arange(ATTN_BLOCK)[None, :]
    dist = qi + ATTN_BLOCK - ki
    in_window = (dist >= 0) & (dist < ATTN_BLOCK)
    in_window_first = in_window & (ki >= ATTN_BLOCK)
    dist = np.maximum(dist, 0)
    max_exact = NUM_BUCKETS // 2
    d = np.maximum(dist, 1).astype(np.float64)
    val = np.log(d / max_exact) / math.log(MAX_DISTANCE / max_exact) * (NUM_BUCKETS - max_exact)
    large = max_exact + np.trunc(val).astype(np.int64)
    frac = np.abs(val - np.round(val))[in_window & (dist > max_exact)]
    assert frac.min() > 1e-3
    bucket = np.where(dist < max_exact, dist, np.minimum(large, NUM_BUCKETS - 1))
    return bucket.reshape(-1), in_window_first.reshape(-1), in_window.reshape(-1)


def _bias_kernel(rbt_ref, onehot_ref, mask_ref, o_ref):
    a1, a2, a3 = _split3_bf16(rbt_ref[...])
    lhs = jnp.concatenate([a1, a2, a3], axis=1)
    base = _dot(lhs, onehot_ref[...])
    o_ref[0] = base + mask_ref[0]
    o_ref[1] = base + mask_ref[1]


def _bias_band(rel_bias, *, tn=4096):
    bucket, first, other = _bucket_table()
    n = bucket.shape[0]
    onehot = (np.arange(NUM_BUCKETS)[:, None] == bucket[None, :]).astype(np.float32)
    onehot3 = jnp.asarray(np.concatenate([onehot] * 3, axis=0), dtype=_BF16)
    maskadd = jnp.asarray(np.where(np.stack([first, other])[:, None, :], 0.0, -np.inf), dtype=_F32)
    out = pl.pallas_call(
        _bias_kernel,
        grid=(n // tn,),
        in_specs=[pl.BlockSpec((ATTN_HEADS, NUM_BUCKETS), lambda j: (0, 0)),
                  pl.BlockSpec((3 * NUM_BUCKETS, tn), lambda j: (0, j)),
                  pl.BlockSpec((2, 1, tn), lambda j: (0, 0, j))],
        out_specs=pl.BlockSpec((2, ATTN_HEADS, tn), lambda j: (0, 0, j)),
        out_shape=jax.ShapeDtypeStruct((2, ATTN_HEADS, n), _F32),
        compiler_params=_compiler_params(1),
        name="rel_bias_band",
    )(rel_bias.astype(_F32).T, onehot3, maskadd)
    return out.reshape(2, ATTN_HEADS, 2 * ATTN_BLOCK, ATTN_BLOCK)


def _group_rms(x, blockdiag, gain):
    x2 = x * x
    hi = x2.astype(_BF16)
    lo = (x2 - hi.astype(_F32)).astype(_BF16)
    ss = _dot(hi, blockdiag) + _dot(lo, blockdiag)
    return x * lax.rsqrt(ss * (1.0 / ATTN_HEAD_DIM) + EPS) * gain


def _pair_stack(slab, low_half):
    swapped = pltpu.roll(slab, ATTN_HEAD_DIM, 1)
    zero = jnp.zeros_like(slab)
    even = jnp.concatenate([jnp.where(low_half, slab, zero), jnp.where(low_half, zero, swapped)], axis=0)
    odd = jnp.concatenate([jnp.where(low_half, swapped, zero), jnp.where(low_half, zero, slab)], axis=0)
    return even.astype(_BF16), odd.astype(_BF16)


class _AttnSteps(NamedTuple):
    prologue: Callable[[], None]
    scores: Callable[[int], None]
    finish: Callable[[int], None]
    n_pairs: int


def _attn_steps(sink_ref, q_ref, kvc_ref, kvp_ref, bias_ref, gq_ref, gk_ref, bd_ref, o_ref):
    w = 2 * ATTN_BLOCK
    st = {}

    def prologue():
        bd = bd_ref[...]
        kv = jnp.concatenate([kvp_ref[...], kvc_ref[...]], axis=0)
        kn = _group_rms(kv[:, :ATTN_KV_DIM], bd, gk_ref[...])
        v = kv[:, ATTN_KV_DIM:]
        lane = lax.broadcasted_iota(jnp.int32, (w, V7X_LANES), 1)
        low_half = lane < ATTN_HEAD_DIM
        k_stacks, v_stacks = [], []
        for s in range(ATTN_KV_DIM // V7X_LANES):
            k_stacks.extend(_pair_stack(kn[:, s * V7X_LANES:(s + 1) * V7X_LANES], low_half))
            v_stacks.extend(_pair_stack(v[:, s * V7X_LANES:(s + 1) * V7X_LANES], low_half))
        st["k"], st["v"], st["bd"] = k_stacks, v_stacks, bd

    def kv_head(pair):
        return (pair * V7X_LANES) // (ATTN_GROUP * ATTN_HEAD_DIM)

    def scores(pair):
        s, half = divmod(pair, 2)
        if half == 0:
            q = q_ref[:, s * w:(s + 1) * w]
            st["qn"] = (_group_rms(q, st["bd"], gq_ref[...]) * ATTN_SCALE).astype(_BF16)
        qn = st["qn"][:, half * V7X_LANES:(half + 1) * V7X_LANES]
        st["sc", pair] = _nt_dot(st["k"][kv_head(pair)], qn)

    def finish(pair):
        sc = st.pop(("sc", pair))
        probs = []
        for e in range(2):
            head = 2 * pair + e
            se = sc[e * w:(e + 1) * w] + bias_ref[head]
            sink = sink_ref[head]
            mx = jnp.maximum(jnp.max(se, axis=0, keepdims=True), sink)
            p = jnp.exp(se - mx)
            denom = jnp.sum(p, axis=0, keepdims=True) + jnp.exp(sink - mx)
            probs.append((p * (1.0 / denom)).astype(_BF16))
        o = _tn_dot(jnp.concatenate(probs, axis=0), st["v"][kv_head(pair)])
        o_ref[:, pair * V7X_LANES:(pair + 1) * V7X_LANES] = o.astype(o_ref.dtype)

    return _AttnSteps(prologue, scores, finish, ATTN_HEADS // 2)


def _proj_attn_kernel(a_ref, w_ref, ss_ref, sink_ref, q_ref, kvc_ref, kvp_ref, bias_ref, gq_ref, gk_ref, bd_ref,
                      hg_ref, o_ref):
    attn = _attn_steps(sink_ref, q_ref, kvc_ref, kvp_ref, bias_ref, gq_ref, gk_ref, bd_ref, o_ref)
    kc = a_ref.shape[1] // attn.n_pairs
    attn.prologue()
    attn.scores(0)
    for p in range(attn.n_pairs):
        part = _dot(a_ref[:, p * kc:(p + 1) * kc], w_ref[p * kc:(p + 1) * kc, :])
        if p + 1 < attn.n_pairs:
            attn.scores(p + 1)
        if p == 0:
            hg_ref[...] = part
        elif p + 1 < attn.n_pairs:
            hg_ref[...] += part
        else:
            hg_ref[...] = _scale_rows_rms(hg_ref[...] + part, ss_ref, a_ref.shape[1])
        attn.finish(p)


def _proj_hgrn_with_attention(nx, w_in_b, col_block0, qkv, sinks, bias, q_gain, k_gain, batch, seq, *, tm, tn):
    m, k = nx.xg.shape
    tm = min(tm, m)
    nb = seq // ATTN_BLOCK
    grid = (m // tm, HGRN_IN_DIM // tn)
    nj = grid[1]
    assert grid[0] * nj == batch * nb and k % (ATTN_HEADS // 2) == 0
    kv_col = ATTN_Q_DIM // (2 * ATTN_KV_DIM)
    assert ATTN_Q_DIM % (2 * ATTN_KV_DIM) == 0
    blockdiag = jnp.asarray(np.kron(np.eye(2 * ATTN_BLOCK // ATTN_HEAD_DIM), np.ones((ATTN_HEAD_DIM,) * 2)), _BF16)
    tile = lambda g: jnp.tile(g.reshape(1, ATTN_HEAD_DIM).astype(_F32), (1, 2 * ATTN_BLOCK // ATTN_HEAD_DIM))
    blk = lambda i, j: i * nj + j
    first = lambda i, j: (blk(i, j) % nb) == 0
    return pl.pallas_call(
        _proj_attn_kernel,
        grid=grid,
        in_specs=[
            pl.BlockSpec((tm, k), lambda i, j: (i, 0)),
            pl.BlockSpec((k, tn), lambda i, j: (0, j + col_block0)),
            pl.BlockSpec((tm, V7X_LANES), lambda i, j: (i, 0)),
            pl.BlockSpec(memory_space=pltpu.SMEM),
            pl.BlockSpec((ATTN_BLOCK, ATTN_Q_DIM), lambda i, j: (blk(i, j), 0)),
            pl.BlockSpec((ATTN_BLOCK, 2 * ATTN_KV_DIM), lambda i, j: (blk(i, j), kv_col)),
            pl.BlockSpec((ATTN_BLOCK, 2 * ATTN_KV_DIM),
                         lambda i, j: (jnp.where(first(i, j), blk(i, j), blk(i, j) - 1), kv_col)),
            pl.BlockSpec((None, ATTN_HEADS, 2 * ATTN_BLOCK, ATTN_BLOCK),
                         lambda i, j: (jnp.where(first(i, j), 0, 1), 0, 0, 0)),
            pl.BlockSpec((1, 2 * ATTN_BLOCK), lambda i, j: (0, 0)),
            pl.BlockSpec((1, 2 * ATTN_BLOCK), lambda i, j: (0, 0)),
            pl.BlockSpec((2 * ATTN_BLOCK, 2 * ATTN_BLOCK), lambda i, j: (0, 0)),
        ],
        out_specs=[pl.BlockSpec((tm, tn), lambda i, j: (i, j)),
                   pl.BlockSpec((ATTN_BLOCK, ATTN_Q_DIM), lambda i, j: (blk(i, j), 0))],
        out_shape=[jax.ShapeDtypeStruct((m, HGRN_IN_DIM), _F32),
                   jax.ShapeDtypeStruct((m, ATTN_Q_DIM), _BF16)],
        compiler_params=_compiler_params(2),
        name="proj_hgrn_attn",
    )(nx.xg, w_in_b, nx.ss, sinks.astype(_F32), qkv, qkv, qkv, bias, tile(q_gain), tile(k_gain), blockdiag)


_HG_LEVELS = tuple(HG_CHUNK >> (i + 1) for i in range(int(math.log2(HG_CHUNK))))
_HG_PIPELINE_LEAD = 3
_HG_MATMUL_LEVELS = tuple(m for m in _HG_LEVELS if m < V7X_F32_SUBLANES)


def _hgrn_tables():
    c = HG_CHUNK
    t = np.arange(c)
    blocks = [(t[None, :] <= t[:, None])]
    masks = [np.eye(c, dtype=bool)]
    for m in _HG_LEVELS:
        ref = (t // (2 * m)) * (2 * m) + m - 1
        upper = (t % (2 * m)) >= m
        if m in _HG_MATMUL_LEVELS:
            up_rows = upper[:, None] & (t[None, :] > ref[:, None]) & (t[None, :] <= t[:, None])
            lo_rows = (~upper)[:, None] & (t[None, :] > t[:, None]) & (t[None, :] <= ref[:, None])
            blocks.append(up_rows | lo_rows)
        same = (t[:, None] // (2 * m)) == (t[None, :] // (2 * m))
        masks.append(same & upper[:, None] & (~upper)[None, :])
    sums = np.concatenate(blocks, axis=0).astype(np.float32)
    return np.concatenate([sums] * 3, axis=1), np.stack(masks).astype(np.float32)


def _hgrn_kernel(p_ref, lbl_ref, gain_ref, sums_ref, masks_ref, o_ref, state_ref, *, layer, rows):
    c = HG_CHUNK

    @pl.when(pl.program_id(1) == 0)
    def _():
        state_ref[...] = jnp.zeros_like(state_ref)

    lg = lbl_ref[...]
    e = jnp.exp(lg - jnp.max(lg, axis=0, keepdims=True))
    sm = e / jnp.sum(e, axis=0, keepdims=True)
    lb_all = jnp.zeros_like(sm[0:1])
    for i in range(1, layer + 1):
        lb_all = lb_all + sm[i:i + 1]

    row = lax.broadcasted_iota(jnp.int32, (c, 1), 0)
    gain = gain_ref[...]
    sums = sums_ref[...]

    def chunk_body(ci, carry):
        r0 = pl.multiple_of(ci * c, c)
        st = [dict() for _ in range(HG_HEADS)]

        def seg(h, which):
            return p_ref[pl.ds(r0, c), which * HG_DIM + h * HG_DK:which * HG_DIM + (h + 1) * HG_DK]

        def gate_math(h):
            lb = lb_all[:, h * HG_DK:(h + 1) * HG_DK]
            hq = seg(h, 0)
            forget = lb + (1.0 - lb) * jax.nn.sigmoid(seg(h, 1))
            st[h]["kk"] = 1.0 - forget
            st[h]["qf"] = hq * jax.nn.sigmoid(hq)
            return jnp.concatenate(_split3_bf16(jnp.log(forget)), axis=0)

        def gates(h):
            if h % 2 == 0:
                both = _dot(sums, jnp.concatenate([gate_math(h), gate_math(h + 1)], axis=1))
                st[h]["expo"], st[h + 1]["expo"] = both[:, :HG_DK], both[:, HG_DK:]

        def level_decay(expo, m):
            if m in _HG_MATMUL_LEVELS:
                k = 1 + _HG_MATMUL_LEVELS.index(m)
                return jnp.exp(expo[k * c:(k + 1) * c])
            bcum = expo[0:c]
            blocks = []
            for s0 in range(0, c, 2 * m):
                blocks.append(-jnp.abs(bcum[s0:s0 + 2 * m] - bcum[s0 + m - 1:s0 + m]))
            return jnp.exp(jnp.concatenate(blocks, axis=0))

        def products(h):
            d = st[h]
            qf, kk = d.pop("qf"), d.pop("kk")
            d["v"] = seg(h, 2).astype(_BF16)
            expo = d.pop("expo")
            bcum = expo[0:c]
            e_cum = jnp.exp(bcum)
            e_end = jnp.exp(bcum[c - 1:c] - bcum)
            state_t = state_ref[h]
            d["o_inter"] = _nt_dot((qf * e_cum).astype(_BF16), state_t.astype(_BF16))
            parts = [_nt_dot(qf.astype(_BF16), kk.astype(_BF16))]
            for m in _HG_LEVELS:
                upper = (row & m) != 0
                z = (jnp.where(upper, qf, kk) * level_decay(expo, m)).astype(_BF16)
                parts.append(_nt_dot(z, z))
            d["parts"] = parts
            state_ref[h] = state_t * e_cum[c - 1:c] + _tn_dot(d["v"], (kk * e_end).astype(_BF16))

        def output(h):
            d = st[h]
            parts = d.pop("parts")
            scores = parts[0] * masks_ref[0]
            for li in range(len(_HG_LEVELS)):
                scores = scores + parts[1 + li] * masks_ref[1 + li]
            o = d.pop("o_inter") + _dot(scores.astype(_BF16), d.pop("v"))
            hg = seg(h, 3)
            y = o * lax.rsqrt(jnp.mean(o * o, axis=-1, keepdims=True) + EPS) * gain
            y = y * (hg * jax.nn.sigmoid(hg))
            o_ref[pl.ds(r0, c), h * HG_DV:(h + 1) * HG_DV] = y.astype(o_ref.dtype)

        lead = _HG_PIPELINE_LEAD
        for h in range(-2 * lead, HG_HEADS):
            if 0 <= h + 2 * lead < HG_HEADS:
                gates(h + 2 * lead)
            if 0 <= h + lead < HG_HEADS:
                products(h + lead)
            if 0 <= h:
                output(h)
        return carry

    lax.fori_loop(0, rows // c, chunk_body, 0)


def _hgrn(p, lb_logits, norm_gain, layer, batch, seq, *, rows=256):
    rows = min(rows, seq)
    steps = seq // rows
    depth = lb_logits.shape[0]
    sums, masks = _hgrn_tables()
    return pl.pallas_call(
        functools.partial(_hgrn_kernel, layer=layer, rows=rows),
        grid=(batch, steps),
        in_specs=[
            pl.BlockSpec((rows, HGRN_IN_DIM), lambda b, s: (b * steps + s, 0)),
            pl.BlockSpec((depth, HG_DIM), lambda b, s: (0, 0)),
            pl.BlockSpec((1, HG_DV), lambda b, s: (0, 0)),
            pl.BlockSpec(sums.shape, lambda b, s: (0, 0)),
            pl.BlockSpec(masks.shape, lambda b, s: (0, 0, 0)),
        ],
        out_specs=pl.BlockSpec((rows, HG_DIM), lambda b, s: (b * steps + s, 0)),
        out_shape=jax.ShapeDtypeStruct((batch * seq, HG_DIM), _BF16),
        scratch_shapes=[pltpu.VMEM((HG_HEADS, HG_DV, HG_DK), _F32)],
        compiler_params=_compiler_params(2),
        name="hgrn2_scan",
    )(p, lb_logits.astype(_F32), norm_gain.reshape(1, HG_DV).astype(_F32),
      jnp.asarray(sums, _BF16), jnp.asarray(masks, _F32))


def kernel(x, attn_norm_gain, w_in, q_norm_gain, k_norm_gain, attn_sinks, rel_bias,
           hgrn_lb_logits, hgrn_norm_gain, w_branch, w_out, mlp_norm_gain, w_up, w_down):
    batch, seq, d = x.shape
    depth = w_in.shape[0]
    m = batch * seq
    xf = x.reshape(m, d).astype(_F32)
    bias = _bias_band(rel_bias)
    t = _TILES
    hg_col0 = QKV_DIM // t.proj_tn
    gate_col0 = QKV_DIM + HGRN_IN_DIM
    assert QKV_DIM % t.proj_tn == 0

    w_in_b = w_in[0].astype(_BF16)
    w_br_b = w_branch[0].astype(_BF16)
    w_o_b = w_out[0].astype(_BF16)
    nx = _normed_input(xf, attn_norm_gain[0])
    for l in range(depth):
        qkv, = _matmul(nx.xg, w_in_b, n_out=QKV_DIM, tm=t.tm, tn=t.qkv_tn, out_dtype=_F32, row_ss=nx.ss,
                       name="proj_qkv")
        hgp, o_attn = _proj_hgrn_with_attention(nx, w_in_b, hg_col0, qkv, attn_sinks[l], bias, q_norm_gain[l],
                                                k_norm_gain[l], batch, seq, tm=t.tm, tn=t.proj_tn)
        o_hgrn = _hgrn(hgp, hgrn_lb_logits, hgrn_norm_gain[l], l, batch, seq)
        merged, w_u_b = _merge(nx, o_attn, o_hgrn, w_in_b, w_br_b, gate_col0=gate_col0,
                               tm=t.tm, tn=t.merge_tn, cvts=(_Cvt(w_up, l),))
        xf, xg, ss = _matmul(merged, w_o_b, n_out=d, tm=t.tm, tn=t.proj_tn, out_dtype=_F32, residual=xf,
                             norm_gain=mlp_norm_gain[l], name="out_proj")

        nxt = () if l + 1 == depth else (_Cvt(w_in, l + 1), _Cvt(w_branch, l + 1), _Cvt(w_out, l + 1))
        u, w_d_b, *nxt_b = _matmul(xg, w_u_b, n_out=w_up.shape[2], tm=t.tm, tn=t.up_tn, out_dtype=_BF16,
                                   row_ss=ss, epilogue=_relu2, cvts=(_Cvt(w_down, l),) + nxt, name="mlp_up")
        if l + 1 == depth:
            xf, = _matmul_ktiled_res(u, w_d_b, xf, tm=t.tm, tn=t.down_tn, tk=t.down_tk, name="mlp_down")
        else:
            w_in_b, w_br_b, w_o_b = nxt_b
            xf, xg, ss = _matmul_ktiled_res(u, w_d_b, xf, tm=t.tm, tn=t.down_tn, tk=t.down_tk,
                                            norm_gain=attn_norm_gain[l + 1], name="mlp_down")
            nx = _Normed(xg, ss)
    return xf.reshape(batch, seq, d).astype(x.dtype)
```

```python
import functools
import math
from typing import Callable, NamedTuple

import numpy as np
import jax
import jax.numpy as jnp
from jax import lax
from jax.experimental import pallas as pl
from jax.experimental.pallas import tpu as pltpu

ATTN_HEADS = 32
ATTN_KV_HEADS = 4
ATTN_HEAD_DIM = 64
ATTN_GROUP = ATTN_HEADS // ATTN_KV_HEADS
ATTN_BLOCK = 128
ATTN_SCALE = ATTN_HEAD_DIM ** -0.5
NUM_BUCKETS = 32
MAX_DISTANCE = 128
HG_HEADS = 8
HG_DK = 128
HG_DV = 128
HG_CHUNK = 64
EPS = 1e-6

ATTN_Q_DIM = ATTN_HEADS * ATTN_HEAD_DIM
ATTN_KV_DIM = ATTN_KV_HEADS * ATTN_HEAD_DIM
QKV_DIM = ATTN_Q_DIM + 2 * ATTN_KV_DIM
HG_DIM = HG_HEADS * HG_DK
HGRN_IN_DIM = 4 * HG_DIM

V7X_LANES = 128
V7X_F32_SUBLANES = 8
BF16_SUBLANES = 16
V7X_VMEM_LIMIT_BYTES = 56 * 1024 * 1024

_BF16 = jnp.bfloat16
_F32 = jnp.float32


class _Tiles(NamedTuple):
    tm: int = 1024
    qkv_tn: int = 1280
    proj_tn: int = 512
    merge_tn: int = 256
    up_tn: int = 1024
    down_tn: int = 1024
    down_tk: int = 2048


_TILES = _Tiles()


def _nt_dot(a, b):
    return lax.dot_general(a, b, (((1,), (1,)), ((), ())), preferred_element_type=_F32)


def _tn_dot(a, b):
    return lax.dot_general(a, b, (((0,), (0,)), ((), ())), preferred_element_type=_F32)


def _dot(a, b):
    return jnp.dot(a, b, preferred_element_type=_F32)


def _split3_bf16(x):
    a1 = x.astype(_BF16)
    r1 = x - a1.astype(_F32)
    a2 = r1.astype(_BF16)
    a3 = (r1 - a2.astype(_F32)).astype(_BF16)
    return a1, a2, a3


def _compiler_params(n_grid):
    return pltpu.CompilerParams(
        dimension_semantics=("arbitrary",) * n_grid,
        vmem_limit_bytes=V7X_VMEM_LIMIT_BYTES,
    )


class _Normed(NamedTuple):
    xg: jax.Array
    ss: jax.Array


def _emit_normed(y, first_col_tile, gain_ref, xg_ref, ss_ref):
    xg_ref[...] = (y * gain_ref[...]).astype(xg_ref.dtype)
    part = jnp.broadcast_to(jnp.sum(y * y, axis=1, keepdims=True), ss_ref.shape)

    @pl.when(first_col_tile)
    def _():
        ss_ref[...] = part

    @pl.when(jnp.logical_not(first_col_tile))
    def _():
        ss_ref[...] += part


def _scale_rows_rms(y, ss_ref, d):
    r = lax.rsqrt(ss_ref[...] * (1.0 / d) + EPS)
    return jnp.concatenate([y[:, c * V7X_LANES:(c + 1) * V7X_LANES] * r for c in range(y.shape[1] // V7X_LANES)],
                           axis=1)


def _normed_input_kernel(x_ref, g_ref, xg_ref, ss_ref):
    _emit_normed(x_ref[...], True, g_ref, xg_ref, ss_ref)


def _normed_input(x, gain, *, rows=256):
    m, d = x.shape
    rows = min(rows, m)
    xg, ss = pl.pallas_call(
        _normed_input_kernel,
        grid=(m // rows,),
        in_specs=[pl.BlockSpec((rows, d), lambda i: (i, 0)),
                  pl.BlockSpec((1, d), lambda i: (0, 0))],
        out_specs=[pl.BlockSpec((rows, d), lambda i: (i, 0)),
                   pl.BlockSpec((rows, V7X_LANES), lambda i: (i, 0))],
        out_shape=[jax.ShapeDtypeStruct((m, d), _BF16), jax.ShapeDtypeStruct((m, V7X_LANES), _F32)],
        compiler_params=_compiler_params(1),
        name="normed_input",
    )(x, gain.reshape(1, d).astype(_F32))
    return _Normed(xg, ss)


class _Cvt(NamedTuple):
    stacked: jax.Array
    layer: int


def _cvt_plumbing(cvts, n_steps, linear_step):
    in_specs, out_specs, out_shapes, args = [], [], [], []
    for c in cvts:
        _, rows, cols = c.stacked.shape
        units = rows // BF16_SUBLANES
        assert rows % BF16_SUBLANES == 0
        blocks = max(b for b in range(1, min(units, n_steps) + 1) if units % b == 0)
        brows = rows // blocks
        idx = functools.partial(lambda *g, blocks: (linear_step(*g) * blocks) // n_steps, blocks=blocks)
        in_specs.append(pl.BlockSpec((None, brows, cols),
                                     functools.partial(lambda *g, idx, layer: (layer, idx(*g), 0), idx=idx, layer=c.layer)))
        out_specs.append(pl.BlockSpec((brows, cols), functools.partial(lambda *g, idx: (idx(*g), 0), idx=idx)))
        out_shapes.append(jax.ShapeDtypeStruct((rows, cols), _BF16))
        args.append(c.stacked)
    return in_specs, out_specs, out_shapes, args


def _run_cvts(src_refs, dst_refs):
    for s, d in zip(src_refs, dst_refs):
        d[...] = s[...].astype(d.dtype)


def _relu2(y):
    r = jnp.maximum(y, 0.0)
    return r * r


def _mm_kernel(*refs, epilogue, has_row_ss, has_residual, has_norm, n_cvt):
    refs = list(refs)
    a_ref, w_ref = refs.pop(0), refs.pop(0)
    ss_in_ref = refs.pop(0) if has_row_ss else None
    r_ref = refs.pop(0) if has_residual else None
    gain_ref = refs.pop(0) if has_norm else None
    cvt_src = [refs.pop(0) for _ in range(n_cvt)]
    o_ref = refs.pop(0)
    xg_ref, ss_ref = (refs.pop(0), refs.pop(0)) if has_norm else (None, None)
    cvt_dst = refs
    y = _dot(a_ref[...], w_ref[...])
    if has_row_ss:
        y = _scale_rows_rms(y, ss_in_ref, a_ref.shape[1])
    if epilogue is not None:
        y = epilogue(y)
    if has_residual:
        y = r_ref[...] + y
    o_ref[...] = y.astype(o_ref.dtype)
    _run_cvts(cvt_src, cvt_dst)
    if has_norm:
        _emit_normed(y, pl.program_id(1) == 0, gain_ref, xg_ref, ss_ref)


def _matmul(a, w, *, n_out, col_block0=0, tm, tn, out_dtype, row_ss=None, epilogue=None, residual=None,
            norm_gain=None, cvts=(), name):
    m, k = a.shape
    tm, tn = min(tm, m), min(tn, n_out)
    grid = (m // tm, n_out // tn)
    n_steps = grid[0] * grid[1]
    in_specs = [pl.BlockSpec((tm, k), lambda i, j: (i, 0)),
                pl.BlockSpec((k, tn), lambda i, j: (0, j + col_block0))]
    args = [a, w]
    out_specs = [pl.BlockSpec((tm, tn), lambda i, j: (i, j))]
    out_shapes = [jax.ShapeDtypeStruct((m, n_out), out_dtype)]
    if row_ss is not None:
        in_specs.append(pl.BlockSpec((tm, V7X_LANES), lambda i, j: (i, 0)))
        args.append(row_ss)
    if residual is not None:
        in_specs.append(pl.BlockSpec((tm, tn), lambda i, j: (i, j)))
        args.append(residual)
    if norm_gain is not None:
        in_specs.append(pl.BlockSpec((1, tn), lambda i, j: (0, j)))
        args.append(norm_gain.reshape(1, n_out).astype(_F32))
        out_specs += [pl.BlockSpec((tm, tn), lambda i, j: (i, j)),
                      pl.BlockSpec((tm, V7X_LANES), lambda i, j: (i, 0))]
        out_shapes += [jax.ShapeDtypeStruct((m, n_out), _BF16), jax.ShapeDtypeStruct((m, V7X_LANES), _F32)]
    c_in, c_out, c_shapes, c_args = _cvt_plumbing(cvts, n_steps, lambda i, j: i * grid[1] + j)
    return pl.pallas_call(
        functools.partial(_mm_kernel, epilogue=epilogue, has_row_ss=row_ss is not None,
                          has_residual=residual is not None, has_norm=norm_gain is not None, n_cvt=len(cvts)),
        grid=grid,
        in_specs=in_specs + c_in,
        out_specs=out_specs + c_out,
        out_shape=out_shapes + c_shapes,
        compiler_params=_compiler_params(2),
        name=name,
    )(*args, *c_args)


def _mm_ktiled_res_kernel(*refs, has_norm, n_cvt, n_k):
    refs = list(refs)
    a_ref, w_ref, r_ref = refs.pop(0), refs.pop(0), refs.pop(0)
    gain_ref = refs.pop(0) if has_norm else None
    cvt_src = [refs.pop(0) for _ in range(n_cvt)]
    o_ref = refs.pop(0)
    xg_ref, ss_ref = (refs.pop(0), refs.pop(0)) if has_norm else (None, None)
    cvt_dst = refs
    k = pl.program_id(2)

    def step(first, last):
        y = (r_ref[...] if first else o_ref[...]) + _dot(a_ref[...], w_ref[...])
        o_ref[...] = y
        _run_cvts(cvt_src, cvt_dst)
        if last and has_norm:
            _emit_normed(y, pl.program_id(1) == 0, gain_ref, xg_ref, ss_ref)

    roles = [(True, n_k == 1, k == 0)]
    if n_k > 2 or (n_k == 2 and not has_norm):
        roles.append((False, False, (k > 0) if not has_norm else jnp.logical_and(k > 0, k < n_k - 1)))
    if n_k > 1 and has_norm:
        roles.append((False, True, k == n_k - 1))
    for first, last, cond in roles:
        pl.when(cond)(functools.partial(step, first, last))


def _matmul_ktiled_res(a, w, residual, *, tm, tn, tk, norm_gain=None, cvts=(), name):
    m, k = a.shape
    _, n = w.shape
    tm, tn, tk = min(tm, m), min(tn, n), min(tk, k)
    grid = (m // tm, n // tn, k // tk)
    n_steps = grid[0] * grid[1] * grid[2]
    in_specs = [pl.BlockSpec((tm, tk), lambda i, j, q: (i, q)),
                pl.BlockSpec((tk, tn), lambda i, j, q: (q, j)),
                pl.BlockSpec((tm, tn), lambda i, j, q: (i, j))]
    args = [a, w, residual]
    out_specs = [pl.BlockSpec((tm, tn), lambda i, j, q: (i, j))]
    out_shapes = [jax.ShapeDtypeStruct((m, n), _F32)]
    if norm_gain is not None:
        in_specs.append(pl.BlockSpec((1, tn), lambda i, j, q: (0, j)))
        args.append(norm_gain.reshape(1, n).astype(_F32))
        out_specs += [pl.BlockSpec((tm, tn), lambda i, j, q: (i, j)),
                      pl.BlockSpec((tm, V7X_LANES), lambda i, j, q: (i, 0))]
        out_shapes += [jax.ShapeDtypeStruct((m, n), _BF16), jax.ShapeDtypeStruct((m, V7X_LANES), _F32)]
    c_in, c_out, c_shapes, c_args = _cvt_plumbing(
        cvts, n_steps, lambda i, j, q: (i * grid[1] + j) * grid[2] + q)
    return pl.pallas_call(
        functools.partial(_mm_ktiled_res_kernel, has_norm=norm_gain is not None, n_cvt=len(cvts), n_k=grid[2]),
        grid=grid,
        in_specs=in_specs + c_in,
        out_specs=out_specs + c_out,
        out_shape=out_shapes + c_shapes,
        compiler_params=_compiler_params(3),
        name=name,
    )(*args, *c_args)


def _merge_kernel(*refs, n_cvt):
    h_ref, ss_ref, oa_ref, oh_ref, wga_ref, wgh_ref, wa_ref, wh_ref = refs[:8]
    cvt_src = refs[8:8 + n_cvt]
    o_ref = refs[8 + n_cvt]
    cvt_dst = refs[9 + n_cvt:]
    h = h_ref[...]
    d = h_ref.shape[1]
    ga = _scale_rows_rms(_dot(h, wga_ref[...]), ss_ref, d)
    gh = _scale_rows_rms(_dot(h, wgh_ref[...]), ss_ref, d)
    ba = _dot(oa_ref[...], wa_ref[...])
    bh = _dot(oh_ref[...], wh_ref[...])
    o_ref[...] = (jax.nn.sigmoid(ga) * ba + jax.nn.sigmoid(gh) * bh).astype(o_ref.dtype)
    _run_cvts(cvt_src, cvt_dst)


def _merge(nx, o_attn, o_hgrn, w_in_b, w_branch_b, *, gate_col0, tm, tn, cvts=()):
    m, d = nx.xg.shape
    tm, tn = min(tm, m), min(tn, d)
    n_tiles = d // tn
    assert ATTN_Q_DIM % HG_DIM == 0 and gate_col0 % tn == 0
    g0 = gate_col0 // tn
    grid = (m // tm, n_tiles)
    c_in, c_out, c_shapes, c_args = _cvt_plumbing(cvts, grid[0] * grid[1], lambda i, j: i * n_tiles + j)
    return pl.pallas_call(
        functools.partial(_merge_kernel, n_cvt=len(cvts)),
        grid=grid,
        in_specs=[
            pl.BlockSpec((tm, d), lambda i, j: (i, 0)),
            pl.BlockSpec((tm, V7X_LANES), lambda i, j: (i, 0)),
            pl.BlockSpec((tm, ATTN_Q_DIM), lambda i, j: (i, 0)),
            pl.BlockSpec((tm, HG_DIM), lambda i, j: (i, 0)),
            pl.BlockSpec((d, tn), lambda i, j: (0, g0 + j)),
            pl.BlockSpec((d, tn), lambda i, j: (0, g0 + n_tiles + j)),
            pl.BlockSpec((ATTN_Q_DIM, tn), lambda i, j: (0, j)),
            pl.BlockSpec((HG_DIM, tn), lambda i, j: (ATTN_Q_DIM // HG_DIM, j)),
        ] + c_in,
        out_specs=[pl.BlockSpec((tm, tn), lambda i, j: (i, j))] + c_out,
        out_shape=[jax.ShapeDtypeStruct((m, d), _BF16)] + c_shapes,
        compiler_params=_compiler_params(2),
        name="gated_merge",
    )(nx.xg, nx.ss, o_attn, o_hgrn, w_in_b, w_in_b, w_branch_b, w_branch_b, *c_args)


def _bucket_table():
    ki = np.arange(2 * ATTN_BLOCK)[:, None]
    qi = np.arange(ATTN_BLOCK)[None, :]
    dist = qi + ATTN_BLOCK - ki
    in_window = (dist >= 0) & (dist < ATTN_BLOCK)
    in_window_first = in_window & (ki >= ATTN_BLOCK)
    dist = np.maximum(dist, 0)
    max_exact = NUM_BUCKETS // 2
    d = np.maximum(dist, 1).astype(np.float64)
    val = np.log(d / max_exact) / math.log(MAX_DISTANCE / max_exact) * (NUM_BUCKETS - max_exact)
    large = max_exact + np.trunc(val).astype(np.int64)
    frac = np.abs(val - np.round(val))[in_window & (dist > max_exact)]
    assert frac.min() > 1e-3
    bucket = np.where(dist < max_exact, dist, np.minimum(large, NUM_BUCKETS - 1))
    return bucket.reshape(-1), in_window_first.reshape(-1), in_window.reshape(-1)


def _bias_kernel(rbt_ref, onehot_ref, mask_ref, o_ref):
    a1, a2, a3 = _split3_bf16(rbt_ref[...])
    lhs = jnp.concatenate([a1, a2, a3], axis=1)
    base = _dot(lhs, onehot_ref[...])
    o_ref[0] = base + mask_ref[0]
    o_ref[1] = base + mask_ref[1]


def _bias_band(rel_bias, *, tn=4096):
    bucket, first, other = _bucket_table()
    n = bucket.shape[0]
    onehot = (np.arange(NUM_BUCKETS)[:, None] == bucket[None, :]).astype(np.float32)
    onehot3 = jnp.asarray(np.concatenate([onehot] * 3, axis=0), dtype=_BF16)
    maskadd = jnp.asarray(np.where(np.stack([first, other])[:, None, :], 0.0, -np.inf), dtype=_F32)
    out = pl.pallas_call(
        _bias_kernel,
        grid=(n // tn,),
        in_specs=[pl.BlockSpec((ATTN_HEADS, NUM_BUCKETS), lambda j: (0, 0)),
                  pl.BlockSpec((3 * NUM_BUCKETS, tn), lambda j: (0, j)),
                  pl.BlockSpec((2, 1, tn), lambda j: (0, 0, j))],
        out_specs=pl.BlockSpec((2, ATTN_HEADS, tn), lambda j: (0, 0, j)),
        out_shape=jax.ShapeDtypeStruct((2, ATTN_HEADS, n), _F32),
        compiler_params=_compiler_params(1),
        name="rel_bias_band",
    )(rel_bias.astype(_F32).T, onehot3, maskadd)
    return out.reshape(2, ATTN_HEADS, 2 * ATTN_BLOCK, ATTN_BLOCK)


def _group_rms(x, blockdiag, gain):
    x2 = x * x
    hi = x2.astype(_BF16)
    lo = (x2 - hi.astype(_F32)).astype(_BF16)
    ss = _dot(hi, blockdiag) + _dot(lo, blockdiag)
    return x * lax.rsqrt(ss * (1.0 / ATTN_HEAD_DIM) + EPS) * gain


def _pair_stack(slab, low_half):
    swapped = pltpu.roll(slab, ATTN_HEAD_DIM, 1)
    zero = jnp.zeros_like(slab)
    even = jnp.concatenate([jnp.where(low_half, slab, zero), jnp.where(low_half, zero, swapped)], axis=0)
    odd = jnp.concatenate([jnp.where(low_half, swapped, zero), jnp.where(low_half, zero, slab)], axis=0)
    return even.astype(_BF16), odd.astype(_BF16)


class _AttnSteps(NamedTuple):
    prologue: Callable[[], None]
    scores: Callable[[int], None]
    finish: Callable[[int], None]
    n_pairs: int


def _attn_steps(sink_ref, q_ref, kvc_ref, kvp_ref, bias_ref, gq_ref, gk_ref, bd_ref, o_ref):
    w = 2 * ATTN_BLOCK
    st = {}

    def prologue():
        bd = bd_ref[...]
        kv = jnp.concatenate([kvp_ref[...], kvc_ref[...]], axis=0)
        kn = _group_rms(kv[:, :ATTN_KV_DIM], bd, gk_ref[...])
        v = kv[:, ATTN_KV_DIM:]
        lane = lax.broadcasted_iota(jnp.int32, (w, V7X_LANES), 1)
        low_half = lane < ATTN_HEAD_DIM
        k_stacks, v_stacks = [], []
        for s in range(ATTN_KV_DIM // V7X_LANES):
            k_stacks.extend(_pair_stack(kn[:, s * V7X_LANES:(s + 1) * V7X_LANES], low_half))
            v_stacks.extend(_pair_stack(v[:, s * V7X_LANES:(s + 1) * V7X_LANES], low_half))
        st["k"], st["v"], st["bd"] = k_stacks, v_stacks, bd

    def kv_head(pair):
        return (pair * V7X_LANES) // (ATTN_GROUP * ATTN_HEAD_DIM)

    def scores(pair):
        s, half = divmod(pair, 2)
        if half == 0:
            q = q_ref[:, s * w:(s + 1) * w]
            st["qn"] = (_group_rms(q, st["bd"], gq_ref[...]) * ATTN_SCALE).astype(_BF16)
        qn = st["qn"][:, half * V7X_LANES:(half + 1) * V7X_LANES]
        st["sc", pair] = _nt_dot(st["k"][kv_head(pair)], qn)

    def finish(pair):
        sc = st.pop(("sc", pair))
        probs = []
        for e in range(2):
            head = 2 * pair + e
            se = sc[e * w:(e + 1) * w] + bias_ref[head]
            sink = sink_ref[head]
            mx = jnp.maximum(jnp.max(se, axis=0, keepdims=True), sink)
            p = jnp.exp(se - mx)
            denom = jnp.sum(p, axis=0, keepdims=True) + jnp.exp(sink - mx)
            probs.append((p * (1.0 / denom)).astype(_BF16))
        o = _tn_dot(jnp.concatenate(probs, axis=0), st["v"][kv_head(pair)])
        o_ref[:, pair * V7X_LANES:(pair + 1) * V7X_LANES] = o.astype(o_ref.dtype)

    return _AttnSteps(prologue, scores, finish, ATTN_HEADS // 2)


def _proj_attn_kernel(a_ref, w_ref, ss_ref, sink_ref, q_ref, kvc_ref, kvp_ref, bias_ref, gq_ref, gk_ref, bd_ref,
                      hg_ref, o_ref):
    attn = _attn_steps(sink_ref, q_ref, kvc_ref, kvp_ref, bias_ref, gq_ref, gk_ref, bd_ref, o_ref)
    kc = a_ref.shape[1] // attn.n_pairs
    attn.prologue()
    attn.scores(0)
    for p in range(attn.n_pairs):
        part = _dot(a_ref[:, p * kc:(p + 1) * kc], w_ref[p * kc:(p + 1) * kc, :])
        if p + 1 < attn.n_pairs:
            attn.scores(p + 1)
        if p == 0:
            hg_ref[...] = part
        elif p + 1 < attn.n_pairs:
            hg_ref[...] += part
        else:
            hg_ref[...] = _scale_rows_rms(hg_ref[...] + part, ss_ref, a_ref.shape[1])
        attn.finish(p)


def _proj_hgrn_with_attention(nx, w_in_b, col_block0, qkv, sinks, bias, q_gain, k_gain, batch, seq, *, tm, tn):
    m, k = nx.xg.shape
    tm = min(tm, m)
    nb = seq // ATTN_BLOCK
    grid = (m // tm, HGRN_IN_DIM // tn)
    nj = grid[1]
    assert grid[0] * nj == batch * nb and k % (ATTN_HEADS // 2) == 0
    kv_col = ATTN_Q_DIM // (2 * ATTN_KV_DIM)
    assert ATTN_Q_DIM % (2 * ATTN_KV_DIM) == 0
    blockdiag = jnp.asarray(np.kron(np.eye(2 * ATTN_BLOCK // ATTN_HEAD_DIM), np.ones((ATTN_HEAD_DIM,) * 2)), _BF16)
    tile = lambda g: jnp.tile(g.reshape(1, ATTN_HEAD_DIM).astype(_F32), (1, 2 * ATTN_BLOCK // ATTN_HEAD_DIM))
    blk = lambda i, j: i * nj + j
    first = lambda i, j: (blk(i, j) % nb) == 0
    return pl.pallas_call(
        _proj_attn_kernel,
        grid=grid,
        in_specs=[
            pl.BlockSpec((tm, k), lambda i, j: (i, 0)),
            pl.BlockSpec((k, tn), lambda i, j: (0, j + col_block0)),
            pl.BlockSpec((tm, V7X_LANES), lambda i, j: (i, 0)),
            pl.BlockSpec(memory_space=pltpu.SMEM),
            pl.BlockSpec((ATTN_BLOCK, ATTN_Q_DIM), lambda i, j: (blk(i, j), 0)),
            pl.BlockSpec((ATTN_BLOCK, 2 * ATTN_KV_DIM), lambda i, j: (blk(i, j), kv_col)),
            pl.BlockSpec((ATTN_BLOCK, 2 * ATTN_KV_DIM),
                         lambda i, j: (jnp.where(first(i, j), blk(i, j), blk(i, j) - 1), kv_col)),
            pl.BlockSpec((None, ATTN_HEADS, 2 * ATTN_BLOCK, ATTN_BLOCK),
                         lambda i, j: (jnp.where(first(i, j), 0, 1), 0, 0, 0)),
            pl.BlockSpec((1, 2 * ATTN_BLOCK), lambda i, j: (0, 0)),
            pl.BlockSpec((1, 2 * ATTN_BLOCK), lambda i, j: (0, 0)),
            pl.BlockSpec((2 * ATTN_BLOCK, 2 * ATTN_BLOCK), lambda i, j: (0, 0)),
        ],
        out_specs=[pl.BlockSpec((tm, tn), lambda i, j: (i, j)),
                   pl.BlockSpec((ATTN_BLOCK, ATTN_Q_DIM), lambda i, j: (blk(i, j), 0))],
        out_shape=[jax.ShapeDtypeStruct((m, HGRN_IN_DIM), _F32),
                   jax.ShapeDtypeStruct((m, ATTN_Q_DIM), _BF16)],
        compiler_params=_compiler_params(2),
        name="proj_hgrn_attn",
    )(nx.xg, w_in_b, nx.ss, sinks.astype(_F32), qkv, qkv, qkv, bias, tile(q_gain), tile(k_gain), blockdiag)


_HG_LEVELS = tuple(HG_CHUNK >> (i + 1) for i in range(int(math.log2(HG_CHUNK))))
_HG_PIPELINE_LEAD = 4
_HG_MATMUL_LEVELS = tuple(m for m in _HG_LEVELS if m < V7X_F32_SUBLANES)


def _hgrn_tables():
    c = HG_CHUNK
    t = np.arange(c)
    blocks = [(t[None, :] <= t[:, None])]
    masks = [np.eye(c, dtype=bool)]
    for m in _HG_LEVELS:
        ref = (t // (2 * m)) * (2 * m) + m - 1
        upper = (t % (2 * m)) >= m
        if m in _HG_MATMUL_LEVELS:
            up_rows = upper[:, None] & (t[None, :] > ref[:, None]) & (t[None, :] <= t[:, None])
            lo_rows = (~upper)[:, None] & (t[None, :] > t[:, None]) & (t[None, :] <= ref[:, None])
            blocks.append(up_rows | lo_rows)
        same = (t[:, None] // (2 * m)) == (t[None, :] // (2 * m))
        masks.append(same & upper[:, None] & (~upper)[None, :])
    sums = np.concatenate(blocks, axis=0).astype(np.float32)
    return np.concatenate([sums] * 3, axis=1), np.stack(masks).astype(np.float32)


def _hgrn_kernel(p_ref, lbl_ref, gain_ref, sums_ref, masks_ref, o_ref, state_ref, *, layer, rows):
    c = HG_CHUNK

    @pl.when(pl.program_id(1) == 0)
    def _():
        state_ref[...] = jnp.zeros_like(state_ref)

    lg = lbl_ref[...]
    e = jnp.exp(lg - jnp.max(lg, axis=0, keepdims=True))
    sm = e / jnp.sum(e, axis=0, keepdims=True)
    lb_all = jnp.zeros_like(sm[0:1])
    for i in range(1, layer + 1):
        lb_all = lb_all + sm[i:i + 1]

    row = lax.broadcasted_iota(jnp.int32, (c, 1), 0)
    gain = gain_ref[...]
    sums = sums_ref[...]

    def chunk_body(ci, carry):
        r0 = pl.multiple_of(ci * c, c)
        st = [dict() for _ in range(HG_HEADS)]

        def seg(h, which):
            return p_ref[pl.ds(r0, c), which * HG_DIM + h * HG_DK:which * HG_DIM + (h + 1) * HG_DK]

        def gate_math(h):
            lb = lb_all[:, h * HG_DK:(h + 1) * HG_DK]
            hq = seg(h, 0)
            forget = lb + (1.0 - lb) * jax.nn.sigmoid(seg(h, 1))
            st[h]["kk"] = 1.0 - forget
            st[h]["qf"] = hq * jax.nn.sigmoid(hq)
            return jnp.concatenate(_split3_bf16(jnp.log(forget)), axis=0)

        def gates(h):
            if h % 2 == 0:
                both = _dot(sums, jnp.concatenate([gate_math(h), gate_math(h + 1)], axis=1))
                st[h]["expo"], st[h + 1]["expo"] = both[:, :HG_DK], both[:, HG_DK:]

        def level_decay(expo, m):
            if m in _HG_MATMUL_LEVELS:
                k = 1 + _HG_MATMUL_LEVELS.index(m)
                return jnp.exp(expo[k * c:(k + 1) * c])
            bcum = expo[0:c]
            blocks = []
            for s0 in range(0, c, 2 * m):
                blocks.append(-jnp.abs(bcum[s0:s0 + 2 * m] - bcum[s0 + m - 1:s0 + m]))
            return jnp.exp(jnp.concatenate(blocks, axis=0))

        def products(h):
            d = st[h]
            qf, kk = d.pop("qf"), d.pop("kk")
            d["v"] = seg(h, 2).astype(_BF16)
            expo = d.pop("expo")
            bcum = expo[0:c]
            e_cum = jnp.exp(bcum)
            e_end = jnp.exp(bcum[c - 1:c] - bcum)
            state_t = state_ref[h]
            d["o_inter"] = _nt_dot((qf * e_cum).astype(_BF16), state_t.astype(_BF16))
            parts = [_nt_dot(qf.astype(_BF16), kk.astype(_BF16))]
            for m in _HG_LEVELS:
                upper = (row & m) != 0
                z = (jnp.where(upper, qf, kk) * level_decay(expo, m)).astype(_BF16)
                parts.append(_nt_dot(z, z))
            d["parts"] = parts
            state_ref[h] = state_t * e_cum[c - 1:c] + _tn_dot(d["v"], (kk * e_end).astype(_BF16))

        def output(h):
            d = st[h]
            parts = d.pop("parts")
            scores = parts[0] * masks_ref[0]
            for li in range(len(_HG_LEVELS)):
                scores = scores + parts[1 + li] * masks_ref[1 + li]
            o = d.pop("o_inter") + _dot(scores.astype(_BF16), d.pop("v"))
            hg = seg(h, 3)
            y = o * lax.rsqrt(jnp.mean(o * o, axis=-1, keepdims=True) + EPS) * gain
            y = y * (hg * jax.nn.sigmoid(hg))
            o_ref[pl.ds(r0, c), h * HG_DV:(h + 1) * HG_DV] = y.astype(o_ref.dtype)

        lead = _HG_PIPELINE_LEAD
        for h in range(-2 * lead, HG_HEADS):
            if 0 <= h + 2 * lead < HG_HEADS:
                gates(h + 2 * lead)
            if 0 <= h + lead < HG_HEADS:
                products(h + lead)
            if 0 <= h:
                output(h)
        return carry

    lax.fori_loop(0, rows // c, chunk_body, 0)


def _hgrn(p, lb_logits, norm_gain, layer, batch, seq, *, rows=512):
    rows = min(rows, seq)
    steps = seq // rows
    depth = lb_logits.shape[0]
    sums, masks = _hgrn_tables()
    return pl.pallas_call(
        functools.partial(_hgrn_kernel, layer=layer, rows=rows),
        grid=(batch, steps),
        in_specs=[
            pl.BlockSpec((rows, HGRN_IN_DIM), lambda b, s: (b * steps + s, 0)),
            pl.BlockSpec((depth, HG_DIM), lambda b, s: (0, 0)),
            pl.BlockSpec((1, HG_DV), lambda b, s: (0, 0)),
            pl.BlockSpec(sums.shape, lambda b, s: (0, 0)),
            pl.BlockSpec(masks.shape, lambda b, s: (0, 0, 0)),
        ],
        out_specs=pl.BlockSpec((rows, HG_DIM), lambda b, s: (b * steps + s, 0)),
        out_shape=jax.ShapeDtypeStruct((batch * seq, HG_DIM), _BF16),
        scratch_shapes=[pltpu.VMEM((HG_HEADS, HG_DV, HG_DK), _F32)],
        compiler_params=_compiler_params(2),
        name="hgrn2_scan",
    )(p, lb_logits.astype(_F32), norm_gain.reshape(1, HG_DV).astype(_F32),
      jnp.asarray(sums, _BF16), jnp.asarray(masks, _F32))


def kernel(x, attn_norm_gain, w_in, q_norm_gain, k_norm_gain, attn_sinks, rel_bias,
           hgrn_lb_logits, hgrn_norm_gain, w_branch, w_out, mlp_norm_gain, w_up, w_down):
    batch, seq, d = x.shape
    depth = w_in.shape[0]
    m = batch * seq
    xf = x.reshape(m, d).astype(_F32)
    bias = _bias_band(rel_bias)
    t = _TILES
    hg_col0 = QKV_DIM // t.proj_tn
    gate_col0 = QKV_DIM + HGRN_IN_DIM
    assert QKV_DIM % t.proj_tn == 0

    w_in_b = w_in[0].astype(_BF16)
    w_br_b = w_branch[0].astype(_BF16)
    w_o_b = w_out[0].astype(_BF16)
    nx = _normed_input(xf, attn_norm_gain[0])
    for l in range(depth):
        qkv, = _matmul(nx.xg, w_in_b, n_out=QKV_DIM, tm=t.tm, tn=t.qkv_tn, out_dtype=_F32, row_ss=nx.ss,
                       name="proj_qkv")
        hgp, o_attn = _proj_hgrn_with_attention(nx, w_in_b, hg_col0, qkv, attn_sinks[l], bias, q_norm_gain[l],
                                                k_norm_gain[l], batch, seq, tm=t.tm, tn=t.proj_tn)
        o_hgrn = _hgrn(hgp, hgrn_lb_logits, hgrn_norm_gain[l], l, batch, seq)
        merged, w_u_b = _merge(nx, o_attn, o_hgrn, w_in_b, w_br_b, gate_col0=gate_col0,
                               tm=t.tm, tn=t.merge_tn, cvts=(_Cvt(w_up, l),))
        xf, xg, ss = _matmul(merged, w_o_b, n_out=d, tm=t.tm, tn=t.proj_tn, out_dtype=_F32, residual=xf,
                             norm_gain=mlp_norm_gain[l], name="out_proj")

        u, w_d_b = _matmul(xg, w_u_b, n_out=w_up.shape[2], tm=t.tm, tn=t.up_tn, out_dtype=_BF16,
                           row_ss=ss, epilogue=_relu2, cvts=(_Cvt(w_down, l),), name="mlp_up")
        if l + 1 == depth:
            xf, = _matmul_ktiled_res(u, w_d_b, xf, tm=t.tm, tn=t.down_tn, tk=t.down_tk, name="mlp_down")
        else:
            xf, xg, ss, w_in_b, w_br_b, w_o_b = _matmul_ktiled_res(
                u, w_d_b, xf, tm=t.tm, tn=t.down_tn, tk=t.down_tk, norm_gain=attn_norm_gain[l + 1],
                cvts=(_Cvt(w_in, l + 1), _Cvt(w_branch, l + 1), _Cvt(w_out, l + 1)), name="mlp_down")
            nx = _Normed(xg, ss)
    return xf.reshape(batch, seq, d).astype(x.dtype)
```

```python
import functools
import math
from typing import Callable, NamedTuple

import numpy as np
import jax
import jax.numpy as jnp
from jax import lax
from jax.experimental import pallas as pl
from jax.experimental.pallas import tpu as pltpu

ATTN_HEADS = 32
ATTN_KV_HEADS = 4
ATTN_HEAD_DIM = 64
ATTN_GROUP = ATTN_HEADS // ATTN_KV_HEADS
ATTN_BLOCK = 128
ATTN_SCALE = ATTN_HEAD_DIM ** -0.5
NUM_BUCKETS = 32
MAX_DISTANCE = 128
HG_HEADS = 8
HG_DK = 128
HG_DV = 128
HG_CHUNK = 64
EPS = 1e-6

ATTN_Q_DIM = ATTN_HEADS * ATTN_HEAD_DIM
ATTN_KV_DIM = ATTN_KV_HEADS * ATTN_HEAD_DIM
QKV_DIM = ATTN_Q_DIM + 2 * ATTN_KV_DIM
HG_DIM = HG_HEADS * HG_DK
HGRN_IN_DIM = 4 * HG_DIM

V7X_LANES = 128
V7X_F32_SUBLANES = 8
BF16_SUBLANES = 16
V7X_VMEM_LIMIT_BYTES = 56 * 1024 * 1024

_BF16 = jnp.bfloat16
_F32 = jnp.float32


class _Tiles(NamedTuple):
    tm: int = 1024
    qkv_tn: int = 1280
    proj_tn: int = 512
    merge_tn: int = 256
    up_tn: int = 1024
    down_tn: int = 1024
    down_tk: int = 2048


_TILES = _Tiles()


def _nt_dot(a, b):
    return lax.dot_general(a, b, (((1,), (1,)), ((), ())), preferred_element_type=_F32)


def _tn_dot(a, b):
    return lax.dot_general(a, b, (((0,), (0,)), ((), ())), preferred_element_type=_F32)


def _dot(a, b):
    return jnp.dot(a, b, preferred_element_type=_F32)


def _split3_bf16(x):
    a1 = x.astype(_BF16)
    r1 = x - a1.astype(_F32)
    a2 = r1.astype(_BF16)
    a3 = (r1 - a2.astype(_F32)).astype(_BF16)
    return a1, a2, a3


def _compiler_params(n_grid):
    return pltpu.CompilerParams(
        dimension_semantics=("arbitrary",) * n_grid,
        vmem_limit_bytes=V7X_VMEM_LIMIT_BYTES,
    )


class _Normed(NamedTuple):
    xg: jax.Array
    ss: jax.Array


def _emit_normed(y, first_col_tile, gain_ref, xg_ref, ss_ref):
    xg_ref[...] = (y * gain_ref[...]).astype(xg_ref.dtype)
    part = jnp.broadcast_to(jnp.sum(y * y, axis=1, keepdims=True), ss_ref.shape)

    @pl.when(first_col_tile)
    def _():
        ss_ref[...] = part

    @pl.when(jnp.logical_not(first_col_tile))
    def _():
        ss_ref[...] += part


def _scale_rows_rms(y, ss_ref, d):
    r = lax.rsqrt(ss_ref[...] * (1.0 / d) + EPS)
    return jnp.concatenate([y[:, c * V7X_LANES:(c + 1) * V7X_LANES] * r for c in range(y.shape[1] // V7X_LANES)],
                           axis=1)


def _normed_input_kernel(x_ref, g_ref, xg_ref, ss_ref):
    _emit_normed(x_ref[...], True, g_ref, xg_ref, ss_ref)


def _normed_input(x, gain, *, rows=256):
    m, d = x.shape
    rows = min(rows, m)
    xg, ss = pl.pallas_call(
        _normed_input_kernel,
        grid=(m // rows,),
        in_specs=[pl.BlockSpec((rows, d), lambda i: (i, 0)),
                  pl.BlockSpec((1, d), lambda i: (0, 0))],
        out_specs=[pl.BlockSpec((rows, d), lambda i: (i, 0)),
                   pl.BlockSpec((rows, V7X_LANES), lambda i: (i, 0))],
        out_shape=[jax.ShapeDtypeStruct((m, d), _BF16), jax.ShapeDtypeStruct((m, V7X_LANES), _F32)],
        compiler_params=_compiler_params(1),
        name="normed_input",
    )(x, gain.reshape(1, d).astype(_F32))
    return _Normed(xg, ss)


class _Cvt(NamedTuple):
    stacked: jax.Array
    layer: int


def _cvt_plumbing(cvts, n_steps, linear_step):
    in_specs, out_specs, out_shapes, args = [], [], [], []
    for c in cvts:
        _, rows, cols = c.stacked.shape
        units = rows // BF16_SUBLANES
        assert rows % BF16_SUBLANES == 0
        blocks = max(b for b in range(1, min(units, n_steps) + 1) if units % b == 0)
        brows = rows // blocks
        idx = functools.partial(lambda *g, blocks: (linear_step(*g) * blocks) // n_steps, blocks=blocks)
        in_specs.append(pl.BlockSpec((None, brows, cols),
                                     functools.partial(lambda *g, idx, layer: (layer, idx(*g), 0), idx=idx, layer=c.layer)))
        out_specs.append(pl.BlockSpec((brows, cols), functools.partial(lambda *g, idx: (idx(*g), 0), idx=idx)))
        out_shapes.append(jax.ShapeDtypeStruct((rows, cols), _BF16))
        args.append(c.stacked)
    return in_specs, out_specs, out_shapes, args


def _run_cvts(src_refs, dst_refs):
    for s, d in zip(src_refs, dst_refs):
        d[...] = s[...].astype(d.dtype)


def _relu2(y):
    r = jnp.maximum(y, 0.0)
    return r * r


def _mm_kernel(*refs, epilogue, has_row_ss, has_residual, has_norm, n_cvt):
    refs = list(refs)
    a_ref, w_ref = refs.pop(0), refs.pop(0)
    ss_in_ref = refs.pop(0) if has_row_ss else None
    r_ref = refs.pop(0) if has_residual else None
    gain_ref = refs.pop(0) if has_norm else None
    cvt_src = [refs.pop(0) for _ in range(n_cvt)]
    o_ref = refs.pop(0)
    xg_ref, ss_ref = (refs.pop(0), refs.pop(0)) if has_norm else (None, None)
    cvt_dst = refs
    y = _dot(a_ref[...], w_ref[...])
    if has_row_ss:
        y = _scale_rows_rms(y, ss_in_ref, a_ref.shape[1])
    if epilogue is not None:
        y = epilogue(y)
    if has_residual:
        y = r_ref[...] + y
    o_ref[...] = y.astype(o_ref.dtype)
    _run_cvts(cvt_src, cvt_dst)
    if has_norm:
        _emit_normed(y, pl.program_id(1) == 0, gain_ref, xg_ref, ss_ref)


def _matmul(a, w, *, n_out, col_block0=0, tm, tn, out_dtype, row_ss=None, epilogue=None, residual=None,
            norm_gain=None, cvts=(), name):
    m, k = a.shape
    tm, tn = min(tm, m), min(tn, n_out)
    grid = (m // tm, n_out // tn)
    n_steps = grid[0] * grid[1]
    in_specs = [pl.BlockSpec((tm, k), lambda i, j: (i, 0)),
                pl.BlockSpec((k, tn), lambda i, j: (0, j + col_block0))]
    args = [a, w]
    out_specs = [pl.BlockSpec((tm, tn), lambda i, j: (i, j))]
    out_shapes = [jax.ShapeDtypeStruct((m, n_out), out_dtype)]
    if row_ss is not None:
        in_specs.append(pl.BlockSpec((tm, V7X_LANES), lambda i, j: (i, 0)))
        args.append(row_ss)
    if residual is not None:
        in_specs.append(pl.BlockSpec((tm, tn), lambda i, j: (i, j)))
        args.append(residual)
    if norm_gain is not None:
        in_specs.append(pl.BlockSpec((1, tn), lambda i, j: (0, j)))
        args.append(norm_gain.reshape(1, n_out).astype(_F32))
        out_specs += [pl.BlockSpec((tm, tn), lambda i, j: (i, j)),
                      pl.BlockSpec((tm, V7X_LANES), lambda i, j: (i, 0))]
        out_shapes += [jax.ShapeDtypeStruct((m, n_out), _BF16), jax.ShapeDtypeStruct((m, V7X_LANES), _F32)]
    c_in, c_out, c_shapes, c_args = _cvt_plumbing(cvts, n_steps, lambda i, j: i * grid[1] + j)
    return pl.pallas_call(
        functools.partial(_mm_kernel, epilogue=epilogue, has_row_ss=row_ss is not None,
                          has_residual=residual is not None, has_norm=norm_gain is not None, n_cvt=len(cvts)),
        grid=grid,
        in_specs=in_specs + c_in,
        out_specs=out_specs + c_out,
        out_shape=out_shapes + c_shapes,
        compiler_params=_compiler_params(2),
        name=name,
    )(*args, *c_args)


def _mm_ktiled_res_kernel(*refs, has_norm, n_cvt, n_k):
    refs = list(refs)
    a_ref, w_ref, r_ref = refs.pop(0), refs.pop(0), refs.pop(0)
    gain_ref = refs.pop(0) if has_norm else None
    cvt_src = [refs.pop(0) for _ in range(n_cvt)]
    o_ref = refs.pop(0)
    xg_ref, ss_ref = (refs.pop(0), refs.pop(0)) if has_norm else (None, None)
    cvt_dst = refs
    k = pl.program_id(2)

    def step(first, last):
        y = (r_ref[...] if first else o_ref[...]) + _dot(a_ref[...], w_ref[...])
        o_ref[...] = y
        _run_cvts(cvt_src, cvt_dst)
        if last and has_norm:
            _emit_normed(y, pl.program_id(1) == 0, gain_ref, xg_ref, ss_ref)

    roles = [(True, n_k == 1, k == 0)]
    if n_k > 2 or (n_k == 2 and not has_norm):
        roles.append((False, False, (k > 0) if not has_norm else jnp.logical_and(k > 0, k < n_k - 1)))
    if n_k > 1 and has_norm:
        roles.append((False, True, k == n_k - 1))
    for first, last, cond in roles:
        pl.when(cond)(functools.partial(step, first, last))


def _matmul_ktiled_res(a, w, residual, *, tm, tn, tk, norm_gain=None, cvts=(), name):
    m, k = a.shape
    _, n = w.shape
    tm, tn, tk = min(tm, m), min(tn, n), min(tk, k)
    grid = (m // tm, n // tn, k // tk)
    n_steps = grid[0] * grid[1] * grid[2]
    in_specs = [pl.BlockSpec((tm, tk), lambda i, j, q: (i, q)),
                pl.BlockSpec((tk, tn), lambda i, j, q: (q, j)),
                pl.BlockSpec((tm, tn), lambda i, j, q: (i, j))]
    args = [a, w, residual]
    out_specs = [pl.BlockSpec((tm, tn), lambda i, j, q: (i, j))]
    out_shapes = [jax.ShapeDtypeStruct((m, n), _F32)]
    if norm_gain is not None:
        in_specs.append(pl.BlockSpec((1, tn), lambda i, j, q: (0, j)))
        args.append(norm_gain.reshape(1, n).astype(_F32))
        out_specs += [pl.BlockSpec((tm, tn), lambda i, j, q: (i, j)),
                      pl.BlockSpec((tm, V7X_LANES), lambda i, j, q: (i, 0))]
        out_shapes += [jax.ShapeDtypeStruct((m, n), _BF16), jax.ShapeDtypeStruct((m, V7X_LANES), _F32)]
    c_in, c_out, c_shapes, c_args = _cvt_plumbing(
        cvts, n_steps, lambda i, j, q: (i * grid[1] + j) * grid[2] + q)
    return pl.pallas_call(
        functools.partial(_mm_ktiled_res_kernel, has_norm=norm_gain is not None, n_cvt=len(cvts), n_k=grid[2]),
        grid=grid,
        in_specs=in_specs + c_in,
        out_specs=out_specs + c_out,
        out_shape=out_shapes + c_shapes,
        compiler_params=_compiler_params(3),
        name=name,
    )(*args, *c_args)


def _merge_kernel(*refs, n_cvt):
    h_ref, ss_ref, oa_ref, oh_ref, wga_ref, wgh_ref, wa_ref, wh_ref = refs[:8]
    cvt_src = refs[8:8 + n_cvt]
    o_ref = refs[8 + n_cvt]
    cvt_dst = refs[9 + n_cvt:]
    h = h_ref[...]
    d = h_ref.shape[1]
    ga = _scale_rows_rms(_dot(h, wga_ref[...]), ss_ref, d)
    gh = _scale_rows_rms(_dot(h, wgh_ref[...]), ss_ref, d)
    ba = _dot(oa_ref[...], wa_ref[...])
    bh = _dot(oh_ref[...], wh_ref[...])
    o_ref[...] = (jax.nn.sigmoid(ga) * ba + jax.nn.sigmoid(gh) * bh).astype(o_ref.dtype)
    _run_cvts(cvt_src, cvt_dst)


def _merge(nx, o_attn, o_hgrn, w_in_b, w_branch_b, *, gate_col0, tm, tn, cvts=()):
    m, d = nx.xg.shape
    tm, tn = min(tm, m), min(tn, d)
    n_tiles = d // tn
    assert ATTN_Q_DIM % HG_DIM == 0 and gate_col0 % tn == 0
    g0 = gate_col0 // tn
    grid = (m // tm, n_tiles)
    c_in, c_out, c_shapes, c_args = _cvt_plumbing(cvts, grid[0] * grid[1], lambda i, j: i * n_tiles + j)
    return pl.pallas_call(
        functools.partial(_merge_kernel, n_cvt=len(cvts)),
        grid=grid,
        in_specs=[
            pl.BlockSpec((tm, d), lambda i, j: (i, 0)),
            pl.BlockSpec((tm, V7X_LANES), lambda i, j: (i, 0)),
            pl.BlockSpec((tm, ATTN_Q_DIM), lambda i, j: (i, 0)),
            pl.BlockSpec((tm, HG_DIM), lambda i, j: (i, 0)),
            pl.BlockSpec((d, tn), lambda i, j: (0, g0 + j)),
            pl.BlockSpec((d, tn), lambda i, j: (0, g0 + n_tiles + j)),
            pl.BlockSpec((ATTN_Q_DIM, tn), lambda i, j: (0, j)),
            pl.BlockSpec((HG_DIM, tn), lambda i, j: (ATTN_Q_DIM // HG_DIM, j)),
        ] + c_in,
        out_specs=[pl.BlockSpec((tm, tn), lambda i, j: (i, j))] + c_out,
        out_shape=[jax.ShapeDtypeStruct((m, d), _BF16)] + c_shapes,
        compiler_params=_compiler_params(2),
        name="gated_merge",
    )(nx.xg, nx.ss, o_attn, o_hgrn, w_in_b, w_in_b, w_branch_b, w_branch_b, *c_args)


def _bucket_table():
    ki = np.arange(2 * ATTN_BLOCK)[:, None]
    qi = np.arange(ATTN_BLOCK)[None, :]
    dist = qi + ATTN_BLOCK - ki
    in_window = (dist >= 0) & (dist < ATTN_BLOCK)
    in_window_first = in_window & (ki >= ATTN_BLOCK)
    dist = np.maximum(dist, 0)
    max_exact = NUM_BUCKETS // 2
    d = np.maximum(dist, 1).astype(np.float64)
    val = np.log(d / max_exact) / math.log(MAX_DISTANCE / max_exact) * (NUM_BUCKETS - max_exact)
    large = max_exact + np.trunc(val).astype(np.int64)
    frac = np.abs(val - np.round(val))[in_window & (dist > max_exact)]
    assert frac.min() > 1e-3
    bucket = np.where(dist < max_exact, dist, np.minimum(large, NUM_BUCKETS - 1))
    return bucket.reshape(-1), in_window_first.reshape(-1), in_window.reshape(-1)


def _bias_kernel(rbt_ref, onehot_ref, mask_ref, o_ref):
    a1, a2, a3 = _split3_bf16(rbt_ref[...])
    lhs = jnp.concatenate([a1, a2, a3], axis=1)
    base = _dot(lhs, onehot_ref[...])
    o_ref[0] = base + mask_ref[0]
    o_ref[1] = base + mask_ref[1]


def _bias_band(rel_bias, *, tn=4096):
    bucket, first, other = _bucket_table()
    n = bucket.shape[0]
    onehot = (np.arange(NUM_BUCKETS)[:, None] == bucket[None, :]).astype(np.float32)
    onehot3 = jnp.asarray(np.concatenate([onehot] * 3, axis=0), dtype=_BF16)
    maskadd = jnp.asarray(np.where(np.stack([first, other])[:, None, :], 0.0, -np.inf), dtype=_F32)
    out = pl.pallas_call(
        _bias_kernel,
        grid=(n // tn,),
        in_specs=[pl.BlockSpec((ATTN_HEADS, NUM_BUCKETS), lambda j: (0, 0)),
                  pl.BlockSpec((3 * NUM_BUCKETS, tn), lambda j: (0, j)),
                  pl.BlockSpec((2, 1, tn), lambda j: (0, 0, j))],
        out_specs=pl.BlockSpec((2, ATTN_HEADS, tn), lambda j: (0, 0, j)),
        out_shape=jax.ShapeDtypeStruct((2, ATTN_HEADS, n), _F32),
        compiler_params=_compiler_params(1),
        name="rel_bias_band",
    )(rel_bias.astype(_F32).T, onehot3, maskadd)
    return out.reshape(2, ATTN_HEADS, 2 * ATTN_BLOCK, ATTN_BLOCK)


def _group_rms(x, blockdiag, gain):
    x2 = x * x
    hi = x2.astype(_BF16)
    lo = (x2 - hi.astype(_F32)).astype(_BF16)
    ss = _dot(hi, blockdiag) + _dot(lo, blockdiag)
    return x * lax.rsqrt(ss * (1.0 / ATTN_HEAD_DIM) + EPS) * gain


def _pair_stack(slab, low_half):
    swapped = pltpu.roll(slab, ATTN_HEAD_DIM, 1)
    zero = jnp.zeros_like(slab)
    even = jnp.concatenate([jnp.where(low_half, slab, zero), jnp.where(low_half, zero, swapped)], axis=0)
    odd = jnp.concatenate([jnp.where(low_half, swapped, zero), jnp.where(low_half, zero, slab)], axis=0)
    return even.astype(_BF16), odd.astype(_BF16)


class _AttnSteps(NamedTuple):
    prologue: Callable[[], None]
    scores: Callable[[int], None]
    finish: Callable[[int], None]
    n_pairs: int


def _attn_steps(sink_ref, q_ref, kvc_ref, kvp_ref, bias_ref, gq_ref, gk_ref, bd_ref, o_ref):
    w = 2 * ATTN_BLOCK
    st = {}

    def prologue():
        bd = bd_ref[...]
        kv = jnp.concatenate([kvp_ref[...], kvc_ref[...]], axis=0)
        kn = _group_rms(kv[:, :ATTN_KV_DIM], bd, gk_ref[...])
        v = kv[:, ATTN_KV_DIM:]
        lane = lax.broadcasted_iota(jnp.int32, (w, V7X_LANES), 1)
        low_half = lane < ATTN_HEAD_DIM
        k_stacks, v_stacks = [], []
        for s in range(ATTN_KV_DIM // V7X_LANES):
            k_stacks.extend(_pair_stack(kn[:, s * V7X_LANES:(s + 1) * V7X_LANES], low_half))
            v_stacks.extend(_pair_stack(v[:, s * V7X_LANES:(s + 1) * V7X_LANES], low_half))
        st["k"], st["v"], st["bd"] = k_stacks, v_stacks, bd

    def kv_head(pair):
        return (pair * V7X_LANES) // (ATTN_GROUP * ATTN_HEAD_DIM)

    def scores(pair):
        s, half = divmod(pair, 2)
        if half == 0:
            q = q_ref[:, s * w:(s + 1) * w]
            st["qn"] = (_group_rms(q, st["bd"], gq_ref[...]) * ATTN_SCALE).astype(_BF16)
        qn = st["qn"][:, half * V7X_LANES:(half + 1) * V7X_LANES]
        st["sc", pair] = _nt_dot(st["k"][kv_head(pair)], qn)

    def finish(pair):
        sc = st.pop(("sc", pair))
        probs = []
        for e in range(2):
            head = 2 * pair + e
            se = sc[e * w:(e + 1) * w] + bias_ref[head]
            sink = sink_ref[head]
            mx = jnp.maximum(jnp.max(se, axis=0, keepdims=True), sink)
            p = jnp.exp(se - mx)
            denom = jnp.sum(p, axis=0, keepdims=True) + jnp.exp(sink - mx)
            probs.append((p * (1.0 / denom)).astype(_BF16))
        o = _tn_dot(jnp.concatenate(probs, axis=0), st["v"][kv_head(pair)])
        o_ref[:, pair * V7X_LANES:(pair + 1) * V7X_LANES] = o.astype(o_ref.dtype)

    return _AttnSteps(prologue, scores, finish, ATTN_HEADS // 2)


def _proj_attn_kernel(*refs, n_cvt):
    a_ref, w_ref, ss_ref, sink_ref, q_ref, kvc_ref, kvp_ref, bias_ref, gq_ref, gk_ref, bd_ref = refs[:11]
    cvt_src = refs[11:11 + n_cvt]
    hg_ref, o_ref = refs[11 + n_cvt:13 + n_cvt]
    cvt_dst = refs[13 + n_cvt:]
    _run_cvts(cvt_src, cvt_dst)
    attn = _attn_steps(sink_ref, q_ref, kvc_ref, kvp_ref, bias_ref, gq_ref, gk_ref, bd_ref, o_ref)
    kc = a_ref.shape[1] // attn.n_pairs
    attn.prologue()
    attn.scores(0)
    for p in range(attn.n_pairs):
        part = _dot(a_ref[:, p * kc:(p + 1) * kc], w_ref[p * kc:(p + 1) * kc, :])
        if p + 1 < attn.n_pairs:
            attn.scores(p + 1)
        if p == 0:
            hg_ref[...] = part
        elif p + 1 < attn.n_pairs:
            hg_ref[...] += part
        else:
            hg_ref[...] = _scale_rows_rms(hg_ref[...] + part, ss_ref, a_ref.shape[1])
        attn.finish(p)


def _proj_hgrn_with_attention(nx, w_in_b, col_block0, qkv, sinks, bias, q_gain, k_gain, batch, seq, *, tm, tn,
                              cvts=()):
    m, k = nx.xg.shape
    tm = min(tm, m)
    nb = seq // ATTN_BLOCK
    grid = (m // tm, HGRN_IN_DIM // tn)
    nj = grid[1]
    assert grid[0] * nj == batch * nb and k % (ATTN_HEADS // 2) == 0
    kv_col = ATTN_Q_DIM // (2 * ATTN_KV_DIM)
    assert ATTN_Q_DIM % (2 * ATTN_KV_DIM) == 0
    blockdiag = jnp.asarray(np.kron(np.eye(2 * ATTN_BLOCK // ATTN_HEAD_DIM), np.ones((ATTN_HEAD_DIM,) * 2)), _BF16)
    tile = lambda g: jnp.tile(g.reshape(1, ATTN_HEAD_DIM).astype(_F32), (1, 2 * ATTN_BLOCK // ATTN_HEAD_DIM))
    blk = lambda i, j: i * nj + j
    first = lambda i, j: (blk(i, j) % nb) == 0
    c_in, c_out, c_shapes, c_args = _cvt_plumbing(cvts, grid[0] * nj, blk)
    return pl.pallas_call(
        functools.partial(_proj_attn_kernel, n_cvt=len(cvts)),
        grid=grid,
        in_specs=[
            pl.BlockSpec((tm, k), lambda i, j: (i, 0)),
            pl.BlockSpec((k, tn), lambda i, j: (0, j + col_block0)),
            pl.BlockSpec((tm, V7X_LANES), lambda i, j: (i, 0)),
            pl.BlockSpec(memory_space=pltpu.SMEM),
            pl.BlockSpec((ATTN_BLOCK, ATTN_Q_DIM), lambda i, j: (blk(i, j), 0)),
            pl.BlockSpec((ATTN_BLOCK, 2 * ATTN_KV_DIM), lambda i, j: (blk(i, j), kv_col)),
            pl.BlockSpec((ATTN_BLOCK, 2 * ATTN_KV_DIM),
                         lambda i, j: (jnp.where(first(i, j), blk(i, j), blk(i, j) - 1), kv_col)),
            pl.BlockSpec((None, ATTN_HEADS, 2 * ATTN_BLOCK, ATTN_BLOCK),
                         lambda i, j: (jnp.where(first(i, j), 0, 1), 0, 0, 0)),
            pl.BlockSpec((1, 2 * ATTN_BLOCK), lambda i, j: (0, 0)),
            pl.BlockSpec((1, 2 * ATTN_BLOCK), lambda i, j: (0, 0)),
            pl.BlockSpec((2 * ATTN_BLOCK, 2 * ATTN_BLOCK), lambda i, j: (0, 0)),
        ] + c_in,
        out_specs=[pl.BlockSpec((tm, tn), lambda i, j: (i, j)),
                   pl.BlockSpec((ATTN_BLOCK, ATTN_Q_DIM), lambda i, j: (blk(i, j), 0))] + c_out,
        out_shape=[jax.ShapeDtypeStruct((m, HGRN_IN_DIM), _F32),
                   jax.ShapeDtypeStruct((m, ATTN_Q_DIM), _BF16)] + c_shapes,
        compiler_params=_compiler_params(2),
        name="proj_hgrn_attn",
    )(nx.xg, w_in_b, nx.ss, sinks.astype(_F32), qkv, qkv, qkv, bias, tile(q_gain), tile(k_gain), blockdiag,
      *c_args)


_HG_LEVELS = tuple(HG_CHUNK >> (i + 1) for i in range(int(math.log2(HG_CHUNK))))
_HG_PIPELINE_LEAD = 4
_HG_MATMUL_LEVELS = tuple(m for m in _HG_LEVELS if m < V7X_F32_SUBLANES)


def _hgrn_tables():
    c = HG_CHUNK
    t = np.arange(c)
    blocks = [(t[None, :] <= t[:, None])]
    masks = [np.eye(c, dtype=bool)]
    for m in _HG_LEVELS:
        ref = (t // (2 * m)) * (2 * m) + m - 1
        upper = (t % (2 * m)) >= m
        if m in _HG_MATMUL_LEVELS:
            up_rows = upper[:, None] & (t[None, :] > ref[:, None]) & (t[None, :] <= t[:, None])
            lo_rows = (~upper)[:, None] & (t[None, :] > t[:, None]) & (t[None, :] <= ref[:, None])
            blocks.append(up_rows | lo_rows)
        same = (t[:, None] // (2 * m)) == (t[None, :] // (2 * m))
        masks.append(same & upper[:, None] & (~upper)[None, :])
    sums = np.concatenate(blocks, axis=0).astype(np.float32)
    return np.concatenate([sums] * 3, axis=1), np.stack(masks).astype(np.float32)


def _hgrn_kernel(p_ref, lbl_ref, gain_ref, sums_ref, masks_ref, o_ref, state_ref, *, layer, rows):
    c = HG_CHUNK

    @pl.when(pl.program_id(1) == 0)
    def _():
        state_ref[...] = jnp.zeros_like(state_ref)

    lg = lbl_ref[...]
    e = jnp.exp(lg - jnp.max(lg, axis=0, keepdims=True))
    sm = e / jnp.sum(e, axis=0, keepdims=True)
    lb_all = jnp.zeros_like(sm[0:1])
    for i in range(1, layer + 1):
        lb_all = lb_all + sm[i:i + 1]

    row = lax.broadcasted_iota(jnp.int32, (c, 1), 0)
    gain = gain_ref[...]
    sums = sums_ref[...]

    def chunk_body(ci, carry):
        r0 = pl.multiple_of(ci * c, c)
        st = [dict() for _ in range(HG_HEADS)]

        def seg(h, which):
            return p_ref[pl.ds(r0, c), which * HG_DIM + h * HG_DK:which * HG_DIM + (h + 1) * HG_DK]

        def gate_math(h):
            lb = lb_all[:, h * HG_DK:(h + 1) * HG_DK]
            hq = seg(h, 0)
            forget = lb + (1.0 - lb) * jax.nn.sigmoid(seg(h, 1))
            st[h]["kk"] = 1.0 - forget
            st[h]["qf"] = hq * jax.nn.sigmoid(hq)
            return jnp.concatenate(_split3_bf16(jnp.log(forget)), axis=0)

        def gates(h):
            if h % 2 == 0:
                both = _dot(sums, jnp.concatenate([gate_math(h), gate_math(h + 1)], axis=1))
                st[h]["expo"], st[h + 1]["expo"] = both[:, :HG_DK], both[:, HG_DK:]

        def level_decay(expo, m):
            if m in _HG_MATMUL_LEVELS:
                k = 1 + _HG_MATMUL_LEVELS.index(m)
                return jnp.exp(expo[k * c:(k + 1) * c])
            bcum = expo[0:c]
            blocks = []
            for s0 in range(0, c, 2 * m):
                blocks.append(-jnp.abs(bcum[s0:s0 + 2 * m] - bcum[s0 + m - 1:s0 + m]))
            return jnp.exp(jnp.concatenate(blocks, axis=0))

        def products(h):
            d = st[h]
            qf, kk = d.pop("qf"), d.pop("kk")
            d["v"] = seg(h, 2).astype(_BF16)
            expo = d.pop("expo")
            bcum = expo[0:c]
            e_cum = jnp.exp(bcum)
            e_end = jnp.exp(bcum[c - 1:c] - bcum)
            state_t = state_ref[h]
            d["o_inter"] = _nt_dot((qf * e_cum).astype(_BF16), state_t.astype(_BF16))
            parts = [_nt_dot(qf.astype(_BF16), kk.astype(_BF16))]
            for m in _HG_LEVELS:
                upper = (row & m) != 0
                z = (jnp.where(upper, qf, kk) * level_decay(expo, m)).astype(_BF16)
                parts.append(_nt_dot(z, z))
            d["parts"] = parts
            state_ref[h] = state_t * e_cum[c - 1:c] + _tn_dot(d["v"], (kk * e_end).astype(_BF16))

        def output(h):
            d = st[h]
            parts = d.pop("parts")
            scores = parts[0] * masks_ref[0]
            for li in range(len(_HG_LEVELS)):
                scores = scores + parts[1 + li] * masks_ref[1 + li]
            o = d.pop("o_inter") + _dot(scores.astype(_BF16), d.pop("v"))
            hg = seg(h, 3)
            y = o * lax.rsqrt(jnp.mean(o * o, axis=-1, keepdims=True) + EPS) * gain
            y = y * (hg * jax.nn.sigmoid(hg))
            o_ref[pl.ds(r0, c), h * HG_DV:(h + 1) * HG_DV] = y.astype(o_ref.dtype)

        lead = _HG_PIPELINE_LEAD
        for h in range(-2 * lead, HG_HEADS):
            if 0 <= h + 2 * lead < HG_HEADS:
                gates(h + 2 * lead)
            if 0 <= h + lead < HG_HEADS:
                products(h + lead)
            if 0 <= h:
                output(h)
        return carry

    lax.fori_loop(0, rows // c, chunk_body, 0)


def _hgrn(p, lb_logits, norm_gain, layer, batch, seq, *, rows=512):
    rows = min(rows, seq)
    steps = seq // rows
    depth = lb_logits.shape[0]
    sums, masks = _hgrn_tables()
    return pl.pallas_call(
        functools.partial(_hgrn_kernel, layer=layer, rows=rows),
        grid=(batch, steps),
        in_specs=[
            pl.BlockSpec((rows, HGRN_IN_DIM), lambda b, s: (b * steps + s, 0)),
            pl.BlockSpec((depth, HG_DIM), lambda b, s: (0, 0)),
            pl.BlockSpec((1, HG_DV), lambda b, s: (0, 0)),
            pl.BlockSpec(sums.shape, lambda b, s: (0, 0)),
            pl.BlockSpec(masks.shape, lambda b, s: (0, 0, 0)),
        ],
        out_specs=pl.BlockSpec((rows, HG_DIM), lambda b, s: (b * steps + s, 0)),
        out_shape=jax.ShapeDtypeStruct((batch * seq, HG_DIM), _BF16),
        scratch_shapes=[pltpu.VMEM((HG_HEADS, HG_DV, HG_DK), _F32)],
        compiler_params=_compiler_params(2),
        name="hgrn2_scan",
    )(p, lb_logits.astype(_F32), norm_gain.reshape(1, HG_DV).astype(_F32),
      jnp.asarray(sums, _BF16), jnp.asarray(masks, _F32))


def kernel(x, attn_norm_gain, w_in, q_norm_gain, k_norm_gain, attn_sinks, rel_bias,
           hgrn_lb_logits, hgrn_norm_gain, w_branch, w_out, mlp_norm_gain, w_up, w_down):
    batch, seq, d = x.shape
    depth = w_in.shape[0]
    m = batch * seq
    xf = x.reshape(m, d).astype(_F32)
    bias = _bias_band(rel_bias)
    t = _TILES
    hg_col0 = QKV_DIM // t.proj_tn
    gate_col0 = QKV_DIM + HGRN_IN_DIM
    assert QKV_DIM % t.proj_tn == 0

    w_in_b = w_in[0].astype(_BF16)
    nx = _normed_input(xf, attn_norm_gain[0])
    for l in range(depth):
        qkv, = _matmul(nx.xg, w_in_b, n_out=QKV_DIM, tm=t.tm, tn=t.qkv_tn, out_dtype=_F32, row_ss=nx.ss,
                       name="proj_qkv")
        first = (_Cvt(w_branch, 0), _Cvt(w_out, 0)) if l == 0 else ()
        hgp, o_attn, *first_b = _proj_hgrn_with_attention(
            nx, w_in_b, hg_col0, qkv, attn_sinks[l], bias, q_norm_gain[l], k_norm_gain[l], batch, seq,
            tm=t.tm, tn=t.proj_tn, cvts=first)
        if first_b:
            w_br_b, w_o_b = first_b
        o_hgrn = _hgrn(hgp, hgrn_lb_logits, hgrn_norm_gain[l], l, batch, seq)
        merged, w_u_b = _merge(nx, o_attn, o_hgrn, w_in_b, w_br_b, gate_col0=gate_col0,
                               tm=t.tm, tn=t.merge_tn, cvts=(_Cvt(w_up, l),))
        xf, xg, ss = _matmul(merged, w_o_b, n_out=d, tm=t.tm, tn=t.proj_tn, out_dtype=_F32, residual=xf,
                             norm_gain=mlp_norm_gain[l], name="out_proj")

        u, w_d_b = _matmul(xg, w_u_b, n_out=w_up.shape[2], tm=t.tm, tn=t.up_tn, out_dtype=_BF16,
                           row_ss=ss, epilogue=_relu2, cvts=(_Cvt(w_down, l),), name="mlp_up")
        if l + 1 == depth:
            xf, = _matmul_ktiled_res(u, w_d_b, xf, tm=t.tm, tn=t.down_tn, tk=t.down_tk, name="mlp_down")
        else:
            xf, xg, ss, w_in_b, w_br_b, w_o_b = _matmul_ktiled_res(
                u, w_d_b, xf, tm=t.tm, tn=t.down_tn, tk=t.down_tk, norm_gain=attn_norm_gain[l + 1],
                cvts=(_Cvt(w_in, l + 1), _Cvt(w_branch, l + 1), _Cvt(w_out, l + 1)), name="mlp_down")
            nx = _Normed(xg, ss)
    return xf.reshape(batch, seq, d).astype(x.dtype)
```

```python
import functools
import math
from typing import Callable, NamedTuple

import numpy as np
import jax
import jax.numpy as jnp
from jax import lax
from jax.experimental import pallas as pl
from jax.experimental.pallas import tpu as pltpu

ATTN_HEADS = 32
ATTN_KV_HEADS = 4
ATTN_HEAD_DIM = 64
ATTN_GROUP = ATTN_HEADS // ATTN_KV_HEADS
ATTN_BLOCK = 128
ATTN_SCALE = ATTN_HEAD_DIM ** -0.5
NUM_BUCKETS = 32
MAX_DISTANCE = 128
HG_HEADS = 8
HG_DK = 128
HG_DV = 128
HG_CHUNK = 64
EPS = 1e-6

ATTN_Q_DIM = ATTN_HEADS * ATTN_HEAD_DIM
ATTN_KV_DIM = ATTN_KV_HEADS * ATTN_HEAD_DIM
QKV_DIM = ATTN_Q_DIM + 2 * ATTN_KV_DIM
HG_DIM = HG_HEADS * HG_DK
HGRN_IN_DIM = 4 * HG_DIM

V7X_LANES = 128
V7X_F32_SUBLANES = 8
BF16_SUBLANES = 16
V7X_VMEM_LIMIT_BYTES = 56 * 1024 * 1024

_BF16 = jnp.bfloat16
_F32 = jnp.float32


class _Tiles(NamedTuple):
    tm: int = 1024
    qkv_tn: int = 1280
    proj_tn: int = 512
    merge_tn: int = 256
    up_tn: int = 1024
    down_tn: int = 1024
    down_tk: int = 2048


_TILES = _Tiles()


def _nt_dot(a, b):
    return lax.dot_general(a, b, (((1,), (1,)), ((), ())), preferred_element_type=_F32)


def _tn_dot(a, b):
    return lax.dot_general(a, b, (((0,), (0,)), ((), ())), preferred_element_type=_F32)


def _dot(a, b):
    return jnp.dot(a, b, preferred_element_type=_F32)


def _split3_bf16(x):
    a1 = x.astype(_BF16)
    r1 = x - a1.astype(_F32)
    a2 = r1.astype(_BF16)
    a3 = (r1 - a2.astype(_F32)).astype(_BF16)
    return a1, a2, a3


def _compiler_params(n_grid):
    return pltpu.CompilerParams(
        dimension_semantics=("arbitrary",) * n_grid,
        vmem_limit_bytes=V7X_VMEM_LIMIT_BYTES,
    )


class _Normed(NamedTuple):
    xg: jax.Array
    ss: jax.Array


def _emit_normed(y, first_col_tile, gain_ref, xg_ref, ss_ref):
    xg_ref[...] = (y * gain_ref[...]).astype(xg_ref.dtype)
    part = jnp.broadcast_to(jnp.sum(y * y, axis=1, keepdims=True), ss_ref.shape)

    @pl.when(first_col_tile)
    def _():
        ss_ref[...] = part

    @pl.when(jnp.logical_not(first_col_tile))
    def _():
        ss_ref[...] += part


def _scale_rows_rms(y, ss_ref, d):
    r = lax.rsqrt(ss_ref[...] * (1.0 / d) + EPS)
    return jnp.concatenate([y[:, c * V7X_LANES:(c + 1) * V7X_LANES] * r for c in range(y.shape[1] // V7X_LANES)],
                           axis=1)


def _normed_input_kernel(x_ref, g_ref, xg_ref, ss_ref):
    _emit_normed(x_ref[...], True, g_ref, xg_ref, ss_ref)


def _normed_input(x, gain, *, rows=256):
    m, d = x.shape
    rows = min(rows, m)
    xg, ss = pl.pallas_call(
        _normed_input_kernel,
        grid=(m // rows,),
        in_specs=[pl.BlockSpec((rows, d), lambda i: (i, 0)),
                  pl.BlockSpec((1, d), lambda i: (0, 0))],
        out_specs=[pl.BlockSpec((rows, d), lambda i: (i, 0)),
                   pl.BlockSpec((rows, V7X_LANES), lambda i: (i, 0))],
        out_shape=[jax.ShapeDtypeStruct((m, d), _BF16), jax.ShapeDtypeStruct((m, V7X_LANES), _F32)],
        compiler_params=_compiler_params(1),
        name="normed_input",
    )(x, gain.reshape(1, d).astype(_F32))
    return _Normed(xg, ss)


class _Cvt(NamedTuple):
    stacked: jax.Array
    layer: int
    col0: int = 0
    ncols: int = 0
    bcols: int = 0


def _cvt_plumbing(cvts, n_steps, linear_step):
    in_specs, out_specs, out_shapes, args = [], [], [], []
    for c in cvts:
        _, rows, cols = c.stacked.shape
        ncols, bcols = (c.ncols, c.bcols) if c.bcols else (cols, cols)
        assert rows % BF16_SUBLANES == 0 and ncols % bcols == 0 and c.col0 % bcols == 0
        n_cb, cb0 = ncols // bcols, c.col0 // bcols
        units = rows // BF16_SUBLANES
        row_blocks = max(b for b in range(1, units + 1) if units % b == 0 and b * n_cb <= max(n_steps, n_cb))
        brows = rows // row_blocks
        total = row_blocks * n_cb
        assert total <= n_steps

        def rb_cb(*g, total=total, n_cb=n_cb):
            return divmod((linear_step(*g) * total) // n_steps, n_cb)

        in_specs.append(pl.BlockSpec(
            (None, brows, bcols),
            functools.partial(lambda *g, f, layer, cb0: (layer, f(*g)[0], cb0 + f(*g)[1]), f=rb_cb, layer=c.layer, cb0=cb0)))
        out_specs.append(pl.BlockSpec((brows, bcols), functools.partial(lambda *g, f: f(*g), f=rb_cb)))
        out_shapes.append(jax.ShapeDtypeStruct((rows, ncols), _BF16))
        args.append(c.stacked)
    return in_specs, out_specs, out_shapes, args


def _run_cvts(src_refs, dst_refs):
    for s, d in zip(src_refs, dst_refs):
        d[...] = s[...].astype(d.dtype)


def _relu2(y):
    r = jnp.maximum(y, 0.0)
    return r * r


def _mm_kernel(*refs, epilogue, has_row_ss, has_residual, has_norm, n_cvt):
    refs = list(refs)
    a_ref, w_ref = refs.pop(0), refs.pop(0)
    ss_in_ref = refs.pop(0) if has_row_ss else None
    r_ref = refs.pop(0) if has_residual else None
    gain_ref = refs.pop(0) if has_norm else None
    cvt_src = [refs.pop(0) for _ in range(n_cvt)]
    o_ref = refs.pop(0)
    xg_ref, ss_ref = (refs.pop(0), refs.pop(0)) if has_norm else (None, None)
    cvt_dst = refs
    y = _dot(a_ref[...], w_ref[...])
    if has_row_ss:
        y = _scale_rows_rms(y, ss_in_ref, a_ref.shape[1])
    if epilogue is not None:
        y = epilogue(y)
    if has_residual:
        y = r_ref[...] + y
    o_ref[...] = y.astype(o_ref.dtype)
    _run_cvts(cvt_src, cvt_dst)
    if has_norm:
        _emit_normed(y, pl.program_id(1) == 0, gain_ref, xg_ref, ss_ref)


def _matmul(a, w, *, n_out, col_block0=0, tm, tn, out_dtype, row_ss=None, epilogue=None, residual=None,
            norm_gain=None, cvts=(), name):
    m, k = a.shape
    tm, tn = min(tm, m), min(tn, n_out)
    grid = (m // tm, n_out // tn)
    n_steps = grid[0] * grid[1]
    in_specs = [pl.BlockSpec((tm, k), lambda i, j: (i, 0)),
                pl.BlockSpec((k, tn), lambda i, j: (0, j + col_block0))]
    args = [a, w]
    out_specs = [pl.BlockSpec((tm, tn), lambda i, j: (i, j))]
    out_shapes = [jax.ShapeDtypeStruct((m, n_out), out_dtype)]
    if row_ss is not None:
        in_specs.append(pl.BlockSpec((tm, V7X_LANES), lambda i, j: (i, 0)))
        args.append(row_ss)
    if residual is not None:
        in_specs.append(pl.BlockSpec((tm, tn), lambda i, j: (i, j)))
        args.append(residual)
    if norm_gain is not None:
        in_specs.append(pl.BlockSpec((1, tn), lambda i, j: (0, j)))
        args.append(norm_gain.reshape(1, n_out).astype(_F32))
        out_specs += [pl.BlockSpec((tm, tn), lambda i, j: (i, j)),
                      pl.BlockSpec((tm, V7X_LANES), lambda i, j: (i, 0))]
        out_shapes += [jax.ShapeDtypeStruct((m, n_out), _BF16), jax.ShapeDtypeStruct((m, V7X_LANES), _F32)]
    c_in, c_out, c_shapes, c_args = _cvt_plumbing(cvts, n_steps, lambda i, j: i * grid[1] + j)
    return pl.pallas_call(
        functools.partial(_mm_kernel, epilogue=epilogue, has_row_ss=row_ss is not None,
                          has_residual=residual is not None, has_norm=norm_gain is not None, n_cvt=len(cvts)),
        grid=grid,
        in_specs=in_specs + c_in,
        out_specs=out_specs + c_out,
        out_shape=out_shapes + c_shapes,
        compiler_params=_compiler_params(2),
        name=name,
    )(*args, *c_args)


def _mm_ktiled_res_kernel(*refs, has_norm, n_cvt, n_k):
    refs = list(refs)
    a_ref, w_ref, r_ref = refs.pop(0), refs.pop(0), refs.pop(0)
    gain_ref = refs.pop(0) if has_norm else None
    cvt_src = [refs.pop(0) for _ in range(n_cvt)]
    o_ref = refs.pop(0)
    xg_ref, ss_ref = (refs.pop(0), refs.pop(0)) if has_norm else (None, None)
    cvt_dst = refs
    k = pl.program_id(2)

    def step(first, last):
        y = (r_ref[...] if first else o_ref[...]) + _dot(a_ref[...], w_ref[...])
        o_ref[...] = y
        _run_cvts(cvt_src, cvt_dst)
        if last and has_norm:
            _emit_normed(y, pl.program_id(1) == 0, gain_ref, xg_ref, ss_ref)

    roles = [(True, n_k == 1, k == 0)]
    if n_k > 2 or (n_k == 2 and not has_norm):
        roles.append((False, False, (k > 0) if not has_norm else jnp.logical_and(k > 0, k < n_k - 1)))
    if n_k > 1 and has_norm:
        roles.append((False, True, k == n_k - 1))
    for first, last, cond in roles:
        pl.when(cond)(functools.partial(step, first, last))


def _matmul_ktiled_res(a, w, residual, *, tm, tn, tk, norm_gain=None, cvts=(), name):
    m, k = a.shape
    _, n = w.shape
    tm, tn, tk = min(tm, m), min(tn, n), min(tk, k)
    grid = (m // tm, n // tn, k // tk)
    n_steps = grid[0] * grid[1] * grid[2]
    in_specs = [pl.BlockSpec((tm, tk), lambda i, j, q: (i, q)),
                pl.BlockSpec((tk, tn), lambda i, j, q: (q, j)),
                pl.BlockSpec((tm, tn), lambda i, j, q: (i, j))]
    args = [a, w, residual]
    out_specs = [pl.BlockSpec((tm, tn), lambda i, j, q: (i, j))]
    out_shapes = [jax.ShapeDtypeStruct((m, n), _F32)]
    if norm_gain is not None:
        in_specs.append(pl.BlockSpec((1, tn), lambda i, j, q: (0, j)))
        args.append(norm_gain.reshape(1, n).astype(_F32))
        out_specs += [pl.BlockSpec((tm, tn), lambda i, j, q: (i, j)),
                      pl.BlockSpec((tm, V7X_LANES), lambda i, j, q: (i, 0))]
        out_shapes += [jax.ShapeDtypeStruct((m, n), _BF16), jax.ShapeDtypeStruct((m, V7X_LANES), _F32)]
    c_in, c_out, c_shapes, c_args = _cvt_plumbing(
        cvts, n_steps, lambda i, j, q: (i * grid[1] + j) * grid[2] + q)
    return pl.pallas_call(
        functools.partial(_mm_ktiled_res_kernel, has_norm=norm_gain is not None, n_cvt=len(cvts), n_k=grid[2]),
        grid=grid,
        in_specs=in_specs + c_in,
        out_specs=out_specs + c_out,
        out_shape=out_shapes + c_shapes,
        compiler_params=_compiler_params(3),
        name=name,
    )(*args, *c_args)


def _merge_kernel(*refs, n_cvt):
    h_ref, ss_ref, oa_ref, oh_ref, wga_ref, wgh_ref, wa_ref, wh_ref = refs[:8]
    cvt_src = refs[8:8 + n_cvt]
    o_ref = refs[8 + n_cvt]
    cvt_dst = refs[9 + n_cvt:]
    h = h_ref[...]
    d = h_ref.shape[1]
    ga = _scale_rows_rms(_dot(h, wga_ref[...]), ss_ref, d)
    gh = _scale_rows_rms(_dot(h, wgh_ref[...]), ss_ref, d)
    ba = _dot(oa_ref[...], wa_ref[...])
    bh = _dot(oh_ref[...], wh_ref[...])
    o_ref[...] = (jax.nn.sigmoid(ga) * ba + jax.nn.sigmoid(gh) * bh).astype(o_ref.dtype)
    _run_cvts(cvt_src, cvt_dst)


def _merge(nx, o_attn, o_hgrn, w_in_b, w_branch_b, *, gate_col0, tm, tn, cvts=()):
    m, d = nx.xg.shape
    tm, tn = min(tm, m), min(tn, d)
    n_tiles = d // tn
    assert ATTN_Q_DIM % HG_DIM == 0 and gate_col0 % tn == 0
    g0 = gate_col0 // tn
    grid = (m // tm, n_tiles)
    c_in, c_out, c_shapes, c_args = _cvt_plumbing(cvts, grid[0] * grid[1], lambda i, j: i * n_tiles + j)
    return pl.pallas_call(
        functools.partial(_merge_kernel, n_cvt=len(cvts)),
        grid=grid,
        in_specs=[
            pl.BlockSpec((tm, d), lambda i, j: (i, 0)),
            pl.BlockSpec((tm, V7X_LANES), lambda i, j: (i, 0)),
            pl.BlockSpec((tm, ATTN_Q_DIM), lambda i, j: (i, 0)),
            pl.BlockSpec((tm, HG_DIM), lambda i, j: (i, 0)),
            pl.BlockSpec((d, tn), lambda i, j: (0, g0 + j)),
            pl.BlockSpec((d, tn), lambda i, j: (0, g0 + n_tiles + j)),
            pl.BlockSpec((ATTN_Q_DIM, tn), lambda i, j: (0, j)),
            pl.BlockSpec((HG_DIM, tn), lambda i, j: (ATTN_Q_DIM // HG_DIM, j)),
        ] + c_in,
        out_specs=[pl.BlockSpec((tm, tn), lambda i, j: (i, j))] + c_out,
        out_shape=[jax.ShapeDtypeStruct((m, d), _BF16)] + c_shapes,
        compiler_params=_compiler_params(2),
        name="gated_merge",
    )(nx.xg, nx.ss, o_attn, o_hgrn, w_in_b, w_in_b, w_branch_b, w_branch_b, *c_args)


def _bucket_table():
    ki = np.arange(2 * ATTN_BLOCK)[:, None]
    qi = np.arange(ATTN_BLOCK)[None, :]
    dist = qi + ATTN_BLOCK - ki
    in_window = (dist >= 0) & (dist < ATTN_BLOCK)
    in_window_first = in_window & (ki >= ATTN_BLOCK)
    dist = np.maximum(dist, 0)
    max_exact = NUM_BUCKETS // 2
    d = np.maximum(dist, 1).astype(np.float64)
    val = np.log(d / max_exact) / math.log(MAX_DISTANCE / max_exact) * (NUM_BUCKETS - max_exact)
    large = max_exact + np.trunc(val).astype(np.int64)
    frac = np.abs(val - np.round(val))[in_window & (dist > max_exact)]
    assert frac.min() > 1e-3
    bucket = np.where(dist < max_exact, dist, np.minimum(large, NUM_BUCKETS - 1))
    return bucket.reshape(-1), in_window_first.reshape(-1), in_window.reshape(-1)


def _bias_kernel(rbt_ref, onehot_ref, mask_ref, o_ref):
    a1, a2, a3 = _split3_bf16(rbt_ref[...])
    lhs = jnp.concatenate([a1, a2, a3], axis=1)
    base = _dot(lhs, onehot_ref[...])
    o_ref[0] = base + mask_ref[0]
    o_ref[1] = base + mask_ref[1]


def _bias_band(rel_bias, *, tn=4096):
    bucket, first, other = _bucket_table()
    n = bucket.shape[0]
    onehot = (np.arange(NUM_BUCKETS)[:, None] == bucket[None, :]).astype(np.float32)
    onehot3 = jnp.asarray(np.concatenate([onehot] * 3, axis=0), dtype=_BF16)
    maskadd = jnp.asarray(np.where(np.stack([first, other])[:, None, :], 0.0, -np.inf), dtype=_F32)
    out = pl.pallas_call(
        _bias_kernel,
        grid=(n // tn,),
        in_specs=[pl.BlockSpec((ATTN_HEADS, NUM_BUCKETS), lambda j: (0, 0)),
                  pl.BlockSpec((3 * NUM_BUCKETS, tn), lambda j: (0, j)),
                  pl.BlockSpec((2, 1, tn), lambda j: (0, 0, j))],
        out_specs=pl.BlockSpec((2, ATTN_HEADS, tn), lambda j: (0, 0, j)),
        out_shape=jax.ShapeDtypeStruct((2, ATTN_HEADS, n), _F32),
        compiler_params=_compiler_params(1),
        name="rel_bias_band",
    )(rel_bias.astype(_F32).T, onehot3, maskadd)
    return out.reshape(2, ATTN_HEADS, 2 * ATTN_BLOCK, ATTN_BLOCK)


def _group_rms(x, blockdiag, gain):
    x2 = x * x
    hi = x2.astype(_BF16)
    lo = (x2 - hi.astype(_F32)).astype(_BF16)
    ss = _dot(hi, blockdiag) + _dot(lo, blockdiag)
    return x * lax.rsqrt(ss * (1.0 / ATTN_HEAD_DIM) + EPS) * gain


def _pair_stack(slab, low_half):
    swapped = pltpu.roll(slab, ATTN_HEAD_DIM, 1)
    zero = jnp.zeros_like(slab)
    even = jnp.concatenate([jnp.where(low_half, slab, zero), jnp.where(low_half, zero, swapped)], axis=0)
    odd = jnp.concatenate([jnp.where(low_half, swapped, zero), jnp.where(low_half, zero, slab)], axis=0)
    return even.astype(_BF16), odd.astype(_BF16)


class _AttnSteps(NamedTuple):
    prologue: Callable[[], None]
    scores: Callable[[int], None]
    finish: Callable[[int], None]
    n_pairs: int


def _attn_steps(sink_ref, q_ref, kvc_ref, kvp_ref, bias_ref, gq_ref, gk_ref, bd_ref, o_ref):
    w = 2 * ATTN_BLOCK
    st = {}

    def prologue():
        bd = bd_ref[...]
        kv = jnp.concatenate([kvp_ref[...], kvc_ref[...]], axis=0)
        kn = _group_rms(kv[:, :ATTN_KV_DIM], bd, gk_ref[...])
        v = kv[:, ATTN_KV_DIM:]
        lane = lax.broadcasted_iota(jnp.int32, (w, V7X_LANES), 1)
        low_half = lane < ATTN_HEAD_DIM
        k_stacks, v_stacks = [], []
        for s in range(ATTN_KV_DIM // V7X_LANES):
            k_stacks.extend(_pair_stack(kn[:, s * V7X_LANES:(s + 1) * V7X_LANES], low_half))
            v_stacks.extend(_pair_stack(v[:, s * V7X_LANES:(s + 1) * V7X_LANES], low_half))
        st["k"], st["v"], st["bd"] = k_stacks, v_stacks, bd

    def kv_head(pair):
        return (pair * V7X_LANES) // (ATTN_GROUP * ATTN_HEAD_DIM)

    def scores(pair):
        s, half = divmod(pair, 2)
        if half == 0:
            q = q_ref[:, s * w:(s + 1) * w]
            st["qn"] = (_group_rms(q, st["bd"], gq_ref[...]) * ATTN_SCALE).astype(_BF16)
        qn = st["qn"][:, half * V7X_LANES:(half + 1) * V7X_LANES]
        st["sc", pair] = _nt_dot(st["k"][kv_head(pair)], qn)

    def finish(pair):
        sc = st.pop(("sc", pair))
        probs = []
        for e in range(2):
            head = 2 * pair + e
            se = sc[e * w:(e + 1) * w] + bias_ref[head]
            sink = sink_ref[head]
            mx = jnp.maximum(jnp.max(se, axis=0, keepdims=True), sink)
            p = jnp.exp(se - mx)
            denom = jnp.sum(p, axis=0, keepdims=True) + jnp.exp(sink - mx)
            probs.append((p * (1.0 / denom)).astype(_BF16))
        o = _tn_dot(jnp.concatenate(probs, axis=0), st["v"][kv_head(pair)])
        o_ref[:, pair * V7X_LANES:(pair + 1) * V7X_LANES] = o.astype(o_ref.dtype)

    return _AttnSteps(prologue, scores, finish, ATTN_HEADS // 2)


def _proj_attn_kernel(*refs, n_cvt):
    a_ref, w_ref, ss_ref, sink_ref, q_ref, kvc_ref, kvp_ref, bias_ref, gq_ref, gk_ref, bd_ref = refs[:11]
    cvt_src = refs[11:11 + n_cvt]
    hg_ref, o_ref = refs[11 + n_cvt:13 + n_cvt]
    cvt_dst = refs[13 + n_cvt:]
    _run_cvts(cvt_src, cvt_dst)
    attn = _attn_steps(sink_ref, q_ref, kvc_ref, kvp_ref, bias_ref, gq_ref, gk_ref, bd_ref, o_ref)
    kc = a_ref.shape[1] // attn.n_pairs
    attn.prologue()
    attn.scores(0)
    for p in range(attn.n_pairs):
        part = _dot(a_ref[:, p * kc:(p + 1) * kc], w_ref[p * kc:(p + 1) * kc, :])
        if p + 1 < attn.n_pairs:
            attn.scores(p + 1)
        if p == 0:
            hg_ref[...] = part
        elif p + 1 < attn.n_pairs:
            hg_ref[...] += part
        else:
            hg_ref[...] = _scale_rows_rms(hg_ref[...] + part, ss_ref, a_ref.shape[1])
        attn.finish(p)


def _proj_hgrn_with_attention(nx, w_in_b, col_block0, qkv, sinks, bias, q_gain, k_gain, batch, seq, *, tm, tn,
                              cvts=()):
    m, k = nx.xg.shape
    tm = min(tm, m)
    nb = seq // ATTN_BLOCK
    grid = (m // tm, HGRN_IN_DIM // tn)
    nj = grid[1]
    assert grid[0] * nj == batch * nb and k % (ATTN_HEADS // 2) == 0
    kv_col = ATTN_Q_DIM // (2 * ATTN_KV_DIM)
    assert ATTN_Q_DIM % (2 * ATTN_KV_DIM) == 0
    blockdiag = jnp.asarray(np.kron(np.eye(2 * ATTN_BLOCK // ATTN_HEAD_DIM), np.ones((ATTN_HEAD_DIM,) * 2)), _BF16)
    tile = lambda g: jnp.tile(g.reshape(1, ATTN_HEAD_DIM).astype(_F32), (1, 2 * ATTN_BLOCK // ATTN_HEAD_DIM))
    blk = lambda i, j: i * nj + j
    first = lambda i, j: (blk(i, j) % nb) == 0
    c_in, c_out, c_shapes, c_args = _cvt_plumbing(cvts, grid[0] * nj, blk)
    return pl.pallas_call(
        functools.partial(_proj_attn_kernel, n_cvt=len(cvts)),
        grid=grid,
        in_specs=[
            pl.BlockSpec((tm, k), lambda i, j: (i, 0)),
            pl.BlockSpec((k, tn), lambda i, j: (0, j + col_block0)),
            pl.BlockSpec((tm, V7X_LANES), lambda i, j: (i, 0)),
            pl.BlockSpec(memory_space=pltpu.SMEM),
            pl.BlockSpec((ATTN_BLOCK, ATTN_Q_DIM), lambda i, j: (blk(i, j), 0)),
            pl.BlockSpec((ATTN_BLOCK, 2 * ATTN_KV_DIM), lambda i, j: (blk(i, j), kv_col)),
            pl.BlockSpec((ATTN_BLOCK, 2 * ATTN_KV_DIM),
                         lambda i, j: (jnp.where(first(i, j), blk(i, j), blk(i, j) - 1), kv_col)),
            pl.BlockSpec((None, ATTN_HEADS, 2 * ATTN_BLOCK, ATTN_BLOCK),
                         lambda i, j: (jnp.where(first(i, j), 0, 1), 0, 0, 0)),
            pl.BlockSpec((1, 2 * ATTN_BLOCK), lambda i, j: (0, 0)),
            pl.BlockSpec((1, 2 * ATTN_BLOCK), lambda i, j: (0, 0)),
            pl.BlockSpec((2 * ATTN_BLOCK, 2 * ATTN_BLOCK), lambda i, j: (0, 0)),
        ] + c_in,
        out_specs=[pl.BlockSpec((tm, tn), lambda i, j: (i, j)),
                   pl.BlockSpec((ATTN_BLOCK, ATTN_Q_DIM), lambda i, j: (blk(i, j), 0))] + c_out,
        out_shape=[jax.ShapeDtypeStruct((m, HGRN_IN_DIM), _F32),
                   jax.ShapeDtypeStruct((m, ATTN_Q_DIM), _BF16)] + c_shapes,
        compiler_params=_compiler_params(2),
        name="proj_hgrn_attn",
    )(nx.xg, w_in_b, nx.ss, sinks.astype(_F32), qkv, qkv, qkv, bias, tile(q_gain), tile(k_gain), blockdiag,
      *c_args)


_HG_LEVELS = tuple(HG_CHUNK >> (i + 1) for i in range(int(math.log2(HG_CHUNK))))
_HG_PIPELINE_LEAD = 4
_HG_MATMUL_LEVELS = tuple(m for m in _HG_LEVELS if m < V7X_F32_SUBLANES)


def _hgrn_tables():
    c = HG_CHUNK
    t = np.arange(c)
    blocks = [(t[None, :] <= t[:, None])]
    masks = [np.eye(c, dtype=bool)]
    for m in _HG_LEVELS:
        ref = (t // (2 * m)) * (2 * m) + m - 1
        upper = (t % (2 * m)) >= m
        if m in _HG_MATMUL_LEVELS:
            up_rows = upper[:, None] & (t[None, :] > ref[:, None]) & (t[None, :] <= t[:, None])
            lo_rows = (~upper)[:, None] & (t[None, :] > t[:, None]) & (t[None, :] <= ref[:, None])
            blocks.append(up_rows | lo_rows)
        same = (t[:, None] // (2 * m)) == (t[None, :] // (2 * m))
        masks.append(same & upper[:, None] & (~upper)[None, :])
    sums = np.concatenate(blocks, axis=0).astype(np.float32)
    return np.concatenate([sums] * 3, axis=1), np.stack(masks).astype(np.float32)


def _hgrn_kernel(p_ref, lbl_ref, gain_ref, sums_ref, masks_ref, o_ref, state_ref, *, layer, rows):
    c = HG_CHUNK

    @pl.when(pl.program_id(1) == 0)
    def _():
        state_ref[...] = jnp.zeros_like(state_ref)

    lg = lbl_ref[...]
    e = jnp.exp(lg - jnp.max(lg, axis=0, keepdims=True))
    sm = e / jnp.sum(e, axis=0, keepdims=True)
    lb_all = jnp.zeros_like(sm[0:1])
    for i in range(1, layer + 1):
        lb_all = lb_all + sm[i:i + 1]

    row = lax.broadcasted_iota(jnp.int32, (c, 1), 0)
    gain = gain_ref[...]
    sums = sums_ref[...]

    def chunk_body(ci, carry):
        r0 = pl.multiple_of(ci * c, c)
        st = [dict() for _ in range(HG_HEADS)]

        def seg(h, which):
            return p_ref[pl.ds(r0, c), which * HG_DIM + h * HG_DK:which * HG_DIM + (h + 1) * HG_DK]

        def gate_math(h):
            lb = lb_all[:, h * HG_DK:(h + 1) * HG_DK]
            hq = seg(h, 0)
            forget = lb + (1.0 - lb) * jax.nn.sigmoid(seg(h, 1))
            st[h]["kk"] = 1.0 - forget
            st[h]["qf"] = hq * jax.nn.sigmoid(hq)
            return jnp.concatenate(_split3_bf16(jnp.log(forget)), axis=0)

        def gates(h):
            if h % 2 == 0:
                both = _dot(sums, jnp.concatenate([gate_math(h), gate_math(h + 1)], axis=1))
                st[h]["expo"], st[h + 1]["expo"] = both[:, :HG_DK], both[:, HG_DK:]

        def level_decay(expo, m):
            if m in _HG_MATMUL_LEVELS:
                k = 1 + _HG_MATMUL_LEVELS.index(m)
                return jnp.exp(expo[k * c:(k + 1) * c])
            bcum = expo[0:c]
            blocks = []
            for s0 in range(0, c, 2 * m):
                blocks.append(-jnp.abs(bcum[s0:s0 + 2 * m] - bcum[s0 + m - 1:s0 + m]))
            return jnp.exp(jnp.concatenate(blocks, axis=0))

        def products(h):
            d = st[h]
            qf, kk = d.pop("qf"), d.pop("kk")
            d["v"] = seg(h, 2).astype(_BF16)
            expo = d.pop("expo")
            bcum = expo[0:c]
            e_cum = jnp.exp(bcum)
            e_end = jnp.exp(bcum[c - 1:c] - bcum)
            state_t = state_ref[h]
            d["o_inter"] = _nt_dot((qf * e_cum).astype(_BF16), state_t.astype(_BF16))
            parts = [_nt_dot(qf.astype(_BF16), kk.astype(_BF16))]
            for m in _HG_LEVELS:
                upper = (row & m) != 0
                z = (jnp.where(upper, qf, kk) * level_decay(expo, m)).astype(_BF16)
                parts.append(_nt_dot(z, z))
            d["parts"] = parts
            state_ref[h] = state_t * e_cum[c - 1:c] + _tn_dot(d["v"], (kk * e_end).astype(_BF16))

        def output(h):
            d = st[h]
            parts = d.pop("parts")
            scores = parts[0] * masks_ref[0]
            for li in range(len(_HG_LEVELS)):
                scores = scores + parts[1 + li] * masks_ref[1 + li]
            o = d.pop("o_inter") + _dot(scores.astype(_BF16), d.pop("v"))
            hg = seg(h, 3)
            y = o * lax.rsqrt(jnp.mean(o * o, axis=-1, keepdims=True) + EPS) * gain
            y = y * (hg * jax.nn.sigmoid(hg))
            o_ref[pl.ds(r0, c), h * HG_DV:(h + 1) * HG_DV] = y.astype(o_ref.dtype)

        lead = _HG_PIPELINE_LEAD
        for h in range(-2 * lead, HG_HEADS):
            if 0 <= h + 2 * lead < HG_HEADS:
                gates(h + 2 * lead)
            if 0 <= h + lead < HG_HEADS:
                products(h + lead)
            if 0 <= h:
                output(h)
        return carry

    lax.fori_loop(0, rows // c, chunk_body, 0)


def _hgrn(p, lb_logits, norm_gain, layer, batch, seq, *, rows=512):
    rows = min(rows, seq)
    steps = seq // rows
    depth = lb_logits.shape[0]
    sums, masks = _hgrn_tables()
    return pl.pallas_call(
        functools.partial(_hgrn_kernel, layer=layer, rows=rows),
        grid=(batch, steps),
        in_specs=[
            pl.BlockSpec((rows, HGRN_IN_DIM), lambda b, s: (b * steps + s, 0)),
            pl.BlockSpec((depth, HG_DIM), lambda b, s: (0, 0)),
            pl.BlockSpec((1, HG_DV), lambda b, s: (0, 0)),
            pl.BlockSpec(sums.shape, lambda b, s: (0, 0)),
            pl.BlockSpec(masks.shape, lambda b, s: (0, 0, 0)),
        ],
        out_specs=pl.BlockSpec((rows, HG_DIM), lambda b, s: (b * steps + s, 0)),
        out_shape=jax.ShapeDtypeStruct((batch * seq, HG_DIM), _BF16),
        scratch_shapes=[pltpu.VMEM((HG_HEADS, HG_DV, HG_DK), _F32)],
        compiler_params=_compiler_params(2),
        name="hgrn2_scan",
    )(p, lb_logits.astype(_F32), norm_gain.reshape(1, HG_DV).astype(_F32),
      jnp.asarray(sums, _BF16), jnp.asarray(masks, _F32))


def kernel(x, attn_norm_gain, w_in, q_norm_gain, k_norm_gain, attn_sinks, rel_bias,
           hgrn_lb_logits, hgrn_norm_gain, w_branch, w_out, mlp_norm_gain, w_up, w_down):
    batch, seq, d = x.shape
    depth = w_in.shape[0]
    m = batch * seq
    xf = x.reshape(m, d).astype(_F32)
    bias = _bias_band(rel_bias)
    t = _TILES
    hg_col0 = QKV_DIM // t.proj_tn
    gate_col0 = QKV_DIM + HGRN_IN_DIM
    assert QKV_DIM % t.proj_tn == 0

    w_in_b = w_in[0, :, :gate_col0].astype(_BF16)
    nx = _normed_input(xf, attn_norm_gain[0])
    for l in range(depth):
        qkv, = _matmul(nx.xg, w_in_b, n_out=QKV_DIM, tm=t.tm, tn=t.qkv_tn, out_dtype=_F32, row_ss=nx.ss,
                       name="proj_qkv")
        first = () if l else (_Cvt(w_branch, 0), _Cvt(w_out, 0),
                              _Cvt(w_in, 0, col0=gate_col0, ncols=2 * d, bcols=t.proj_tn))
        hgp, o_attn, *first_b = _proj_hgrn_with_attention(
            nx, w_in_b, hg_col0, qkv, attn_sinks[l], bias, q_norm_gain[l], k_norm_gain[l], batch, seq,
            tm=t.tm, tn=t.proj_tn, cvts=first)
        if first_b:
            w_br_b, w_o_b, w_gate0_b = first_b
        o_hgrn = _hgrn(hgp, hgrn_lb_logits, hgrn_norm_gain[l], l, batch, seq)
        w_gate_b, g0 = (w_in_b, gate_col0) if l else (w_gate0_b, 0)
        merged, w_u_b = _merge(nx, o_attn, o_hgrn, w_gate_b, w_br_b, gate_col0=g0,
                               tm=t.tm, tn=t.merge_tn, cvts=(_Cvt(w_up, l),))
        xf, xg, ss = _matmul(merged, w_o_b, n_out=d, tm=t.tm, tn=t.proj_tn, out_dtype=_F32, residual=xf,
                             norm_gain=mlp_norm_gain[l], name="out_proj")

        u, w_d_b = _matmul(xg, w_u_b, n_out=w_up.shape[2], tm=t.tm, tn=t.up_tn, out_dtype=_BF16,
                           row_ss=ss, epilogue=_relu2, cvts=(_Cvt(w_down, l),), name="mlp_up")
        if l + 1 == depth:
            xf, = _matmul_ktiled_res(u, w_d_b, xf, tm=t.tm, tn=t.down_tn, tk=t.down_tk, name="mlp_down")
        else:
            xf, xg, ss, w_in_b, w_br_b, w_o_b = _matmul_ktiled_res(
                u, w_d_b, xf, tm=t.tm, tn=t.down_tn, tk=t.down_tk, norm_gain=attn_norm_gain[l + 1],
                cvts=(_Cvt(w_in, l + 1), _Cvt(w_branch, l + 1), _Cvt(w_out, l + 1)), name="mlp_down")
            nx = _Normed(xg, ss)
    return xf.reshape(batch, seq, d).astype(x.dtype)
```

```python
import functools
import math
from typing import Callable, NamedTuple

import numpy as np
import jax
import jax.numpy as jnp
from jax import lax
from jax.experimental import pallas as pl
from jax.experimental.pallas import tpu as pltpu

ATTN_HEADS = 32
ATTN_KV_HEADS = 4
ATTN_HEAD_DIM = 64
ATTN_GROUP = ATTN_HEADS // ATTN_KV_HEADS
ATTN_BLOCK = 128
ATTN_SCALE = ATTN_HEAD_DIM ** -0.5
NUM_BUCKETS = 32
MAX_DISTANCE = 128
HG_HEADS = 8
HG_DK = 128
HG_DV = 128
HG_CHUNK = 64
EPS = 1e-6

ATTN_Q_DIM = ATTN_HEADS * ATTN_HEAD_DIM
ATTN_KV_DIM = ATTN_KV_HEADS * ATTN_HEAD_DIM
QKV_DIM = ATTN_Q_DIM + 2 * ATTN_KV_DIM
HG_DIM = HG_HEADS * HG_DK
HGRN_IN_DIM = 4 * HG_DIM

V7X_LANES = 128
V7X_F32_SUBLANES = 8
BF16_SUBLANES = 16
V7X_VMEM_LIMIT_BYTES = 56 * 1024 * 1024

_BF16 = jnp.bfloat16
_F32 = jnp.float32


class _Tiles(NamedTuple):
    tm: int = 1024
    qkv_tn: int = 1280
    proj_tn: int = 512
    merge_tn: int = 256
    up_tn: int = 1024
    down_tn: int = 2048
    down_tk: int = 1024


_TILES = _Tiles()


def _nt_dot(a, b):
    return lax.dot_general(a, b, (((1,), (1,)), ((), ())), preferred_element_type=_F32)


def _tn_dot(a, b):
    return lax.dot_general(a, b, (((0,), (0,)), ((), ())), preferred_element_type=_F32)


def _dot(a, b):
    return jnp.dot(a, b, preferred_element_type=_F32)


def _split3_bf16(x):
    a1 = x.astype(_BF16)
    r1 = x - a1.astype(_F32)
    a2 = r1.astype(_BF16)
    a3 = (r1 - a2.astype(_F32)).astype(_BF16)
    return a1, a2, a3


def _compiler_params(n_grid):
    return pltpu.CompilerParams(
        dimension_semantics=("arbitrary",) * n_grid,
        vmem_limit_bytes=V7X_VMEM_LIMIT_BYTES,
    )


class _Normed(NamedTuple):
    xg: jax.Array
    ss: jax.Array


def _emit_normed(y, first_col_tile, gain_ref, xg_ref, ss_ref):
    xg_ref[...] = (y * gain_ref[...]).astype(xg_ref.dtype)
    part = jnp.broadcast_to(jnp.sum(y * y, axis=1, keepdims=True), ss_ref.shape)

    @pl.when(first_col_tile)
    def _():
        ss_ref[...] = part

    @pl.when(jnp.logical_not(first_col_tile))
    def _():
        ss_ref[...] += part


def _scale_rows_rms(y, ss_ref, d):
    r = lax.rsqrt(ss_ref[...] * (1.0 / d) + EPS)
    return jnp.concatenate([y[:, c * V7X_LANES:(c + 1) * V7X_LANES] * r for c in range(y.shape[1] // V7X_LANES)],
                           axis=1)


def _normed_input_kernel(x_ref, g_ref, xg_ref, ss_ref):
    _emit_normed(x_ref[...], True, g_ref, xg_ref, ss_ref)


def _normed_input(x, gain, *, rows=256):
    m, d = x.shape
    rows = min(rows, m)
    xg, ss = pl.pallas_call(
        _normed_input_kernel,
        grid=(m // rows,),
        in_specs=[pl.BlockSpec((rows, d), lambda i: (i, 0)),
                  pl.BlockSpec((1, d), lambda i: (0, 0))],
        out_specs=[pl.BlockSpec((rows, d), lambda i: (i, 0)),
                   pl.BlockSpec((rows, V7X_LANES), lambda i: (i, 0))],
        out_shape=[jax.ShapeDtypeStruct((m, d), _BF16), jax.ShapeDtypeStruct((m, V7X_LANES), _F32)],
        compiler_params=_compiler_params(1),
        name="normed_input",
    )(x, gain.reshape(1, d).astype(_F32))
    return _Normed(xg, ss)


class _Cvt(NamedTuple):
    stacked: jax.Array
    layer: int
    col0: int = 0
    ncols: int = 0
    bcols: int = 0


def _cvt_plumbing(cvts, n_steps, linear_step):
    in_specs, out_specs, out_shapes, args = [], [], [], []
    for c in cvts:
        _, rows, cols = c.stacked.shape
        ncols, bcols = (c.ncols, c.bcols) if c.bcols else (cols, cols)
        assert rows % BF16_SUBLANES == 0 and ncols % bcols == 0 and c.col0 % bcols == 0
        n_cb, cb0 = ncols // bcols, c.col0 // bcols
        units = rows // BF16_SUBLANES
        row_blocks = max(b for b in range(1, units + 1) if units % b == 0 and b * n_cb <= max(n_steps, n_cb))
        brows = rows // row_blocks
        total = row_blocks * n_cb
        assert total <= n_steps

        def rb_cb(*g, total=total, n_cb=n_cb):
            return divmod((linear_step(*g) * total) // n_steps, n_cb)

        in_specs.append(pl.BlockSpec(
            (None, brows, bcols),
            functools.partial(lambda *g, f, layer, cb0: (layer, f(*g)[0], cb0 + f(*g)[1]), f=rb_cb, layer=c.layer, cb0=cb0)))
        out_specs.append(pl.BlockSpec((brows, bcols), functools.partial(lambda *g, f: f(*g), f=rb_cb)))
        out_shapes.append(jax.ShapeDtypeStruct((rows, ncols), _BF16))
        args.append(c.stacked)
    return in_specs, out_specs, out_shapes, args


def _run_cvts(src_refs, dst_refs):
    for s, d in zip(src_refs, dst_refs):
        d[...] = s[...].astype(d.dtype)


def _relu2(y):
    r = jnp.maximum(y, 0.0)
    return r * r


def _mm_kernel(*refs, epilogue, has_row_ss, has_residual, has_norm, n_cvt):
    refs = list(refs)
    a_ref, w_ref = refs.pop(0), refs.pop(0)
    ss_in_ref = refs.pop(0) if has_row_ss else None
    r_ref = refs.pop(0) if has_residual else None
    gain_ref = refs.pop(0) if has_norm else None
    cvt_src = [refs.pop(0) for _ in range(n_cvt)]
    o_ref = refs.pop(0)
    xg_ref, ss_ref = (refs.pop(0), refs.pop(0)) if has_norm else (None, None)
    cvt_dst = refs
    y = _dot(a_ref[...], w_ref[...])
    if has_row_ss:
        y = _scale_rows_rms(y, ss_in_ref, a_ref.shape[1])
    if epilogue is not None:
        y = epilogue(y)
    if has_residual:
        y = r_ref[...] + y
    o_ref[...] = y.astype(o_ref.dtype)
    _run_cvts(cvt_src, cvt_dst)
    if has_norm:
        _emit_normed(y, pl.program_id(1) == 0, gain_ref, xg_ref, ss_ref)


def _matmul(a, w, *, n_out, col_block0=0, tm, tn, out_dtype, row_ss=None, epilogue=None, residual=None,
            norm_gain=None, cvts=(), name):
    m, k = a.shape
    tm, tn = min(tm, m), min(tn, n_out)
    grid = (m // tm, n_out // tn)
    n_steps = grid[0] * grid[1]
    in_specs = [pl.BlockSpec((tm, k), lambda i, j: (i, 0)),
                pl.BlockSpec((k, tn), lambda i, j: (0, j + col_block0))]
    args = [a, w]
    out_specs = [pl.BlockSpec((tm, tn), lambda i, j: (i, j))]
    out_shapes = [jax.ShapeDtypeStruct((m, n_out), out_dtype)]
    if row_ss is not None:
        in_specs.append(pl.BlockSpec((tm, V7X_LANES), lambda i, j: (i, 0)))
        args.append(row_ss)
    if residual is not None:
        in_specs.append(pl.BlockSpec((tm, tn), lambda i, j: (i, j)))
        args.append(residual)
    if norm_gain is not None:
        in_specs.append(pl.BlockSpec((1, tn), lambda i, j: (0, j)))
        args.append(norm_gain.reshape(1, n_out).astype(_F32))
        out_specs += [pl.BlockSpec((tm, tn), lambda i, j: (i, j)),
                      pl.BlockSpec((tm, V7X_LANES), lambda i, j: (i, 0))]
        out_shapes += [jax.ShapeDtypeStruct((m, n_out), _BF16), jax.ShapeDtypeStruct((m, V7X_LANES), _F32)]
    c_in, c_out, c_shapes, c_args = _cvt_plumbing(cvts, n_steps, lambda i, j: i * grid[1] + j)
    return pl.pallas_call(
        functools.partial(_mm_kernel, epilogue=epilogue, has_row_ss=row_ss is not None,
                          has_residual=residual is not None, has_norm=norm_gain is not None, n_cvt=len(cvts)),
        grid=grid,
        in_specs=in_specs + c_in,
        out_specs=out_specs + c_out,
        out_shape=out_shapes + c_shapes,
        compiler_params=_compiler_params(2),
        name=name,
    )(*args, *c_args)


def _mm_ktiled_res_kernel(*refs, has_norm, n_cvt, n_k):
    refs = list(refs)
    a_ref, w_ref, r_ref = refs.pop(0), refs.pop(0), refs.pop(0)
    gain_ref = refs.pop(0) if has_norm else None
    cvt_src = [refs.pop(0) for _ in range(n_cvt)]
    o_ref = refs.pop(0)
    xg_ref, ss_ref = (refs.pop(0), refs.pop(0)) if has_norm else (None, None)
    cvt_dst = refs
    k = pl.program_id(2)

    def step(first, last):
        y = (r_ref[...] if first else o_ref[...]) + _dot(a_ref[...], w_ref[...])
        o_ref[...] = y
        _run_cvts(cvt_src, cvt_dst)
        if last and has_norm:
            _emit_normed(y, pl.program_id(1) == 0, gain_ref, xg_ref, ss_ref)

    roles = [(True, n_k == 1, k == 0)]
    if n_k > 2 or (n_k == 2 and not has_norm):
        roles.append((False, False, (k > 0) if not has_norm else jnp.logical_and(k > 0, k < n_k - 1)))
    if n_k > 1 and has_norm:
        roles.append((False, True, k == n_k - 1))
    for first, last, cond in roles:
        pl.when(cond)(functools.partial(step, first, last))


def _matmul_ktiled_res(a, w, residual, *, tm, tn, tk, norm_gain=None, cvts=(), name):
    m, k = a.shape
    _, n = w.shape
    tm, tn, tk = min(tm, m), min(tn, n), min(tk, k)
    grid = (m // tm, n // tn, k // tk)
    n_steps = grid[0] * grid[1] * grid[2]
    in_specs = [pl.BlockSpec((tm, tk), lambda i, j, q: (i, q)),
                pl.BlockSpec((tk, tn), lambda i, j, q: (q, j)),
                pl.BlockSpec((tm, tn), lambda i, j, q: (i, j))]
    args = [a, w, residual]
    out_specs = [pl.BlockSpec((tm, tn), lambda i, j, q: (i, j))]
    out_shapes = [jax.ShapeDtypeStruct((m, n), _F32)]
    if norm_gain is not None:
        in_specs.append(pl.BlockSpec((1, tn), lambda i, j, q: (0, j)))
        args.append(norm_gain.reshape(1, n).astype(_F32))
        out_specs += [pl.BlockSpec((tm, tn), lambda i, j, q: (i, j)),
                      pl.BlockSpec((tm, V7X_LANES), lambda i, j, q: (i, 0))]
        out_shapes += [jax.ShapeDtypeStruct((m, n), _BF16), jax.ShapeDtypeStruct((m, V7X_LANES), _F32)]
    c_in, c_out, c_shapes, c_args = _cvt_plumbing(
        cvts, n_steps, lambda i, j, q: (i * grid[1] + j) * grid[2] + q)
    return pl.pallas_call(
        functools.partial(_mm_ktiled_res_kernel, has_norm=norm_gain is not None, n_cvt=len(cvts), n_k=grid[2]),
        grid=grid,
        in_specs=in_specs + c_in,
        out_specs=out_specs + c_out,
        out_shape=out_shapes + c_shapes,
        compiler_params=_compiler_params(3),
        name=name,
    )(*args, *c_args)


def _merge_kernel(*refs, n_cvt):
    h_ref, ss_ref, oa_ref, oh_ref, wga_ref, wgh_ref, wa_ref, wh_ref = refs[:8]
    cvt_src = refs[8:8 + n_cvt]
    o_ref = refs[8 + n_cvt]
    cvt_dst = refs[9 + n_cvt:]
    h = h_ref[...]
    d = h_ref.shape[1]
    ga = _scale_rows_rms(_dot(h, wga_ref[...]), ss_ref, d)
    gh = _scale_rows_rms(_dot(h, wgh_ref[...]), ss_ref, d)
    ba = _dot(oa_ref[...], wa_ref[...])
    bh = _dot(oh_ref[...], wh_ref[...])
    o_ref[...] = (jax.nn.sigmoid(ga) * ba + jax.nn.sigmoid(gh) * bh).astype(o_ref.dtype)
    _run_cvts(cvt_src, cvt_dst)


def _merge(nx, o_attn, o_hgrn, w_in_b, w_branch_b, *, gate_col0, tm, tn, cvts=()):
    m, d = nx.xg.shape
    tm, tn = min(tm, m), min(tn, d)
    n_tiles = d // tn
    assert ATTN_Q_DIM % HG_DIM == 0 and gate_col0 % tn == 0
    g0 = gate_col0 // tn
    grid = (m // tm, n_tiles)
    c_in, c_out, c_shapes, c_args = _cvt_plumbing(cvts, grid[0] * grid[1], lambda i, j: i * n_tiles + j)
    return pl.pallas_call(
        functools.partial(_merge_kernel, n_cvt=len(cvts)),
        grid=grid,
        in_specs=[
            pl.BlockSpec((tm, d), lambda i, j: (i, 0)),
            pl.BlockSpec((tm, V7X_LANES), lambda i, j: (i, 0)),
            pl.BlockSpec((tm, ATTN_Q_DIM), lambda i, j: (i, 0)),
            pl.BlockSpec((tm, HG_DIM), lambda i, j: (i, 0)),
            pl.BlockSpec((d, tn), lambda i, j: (0, g0 + j)),
            pl.BlockSpec((d, tn), lambda i, j: (0, g0 + n_tiles + j)),
            pl.BlockSpec((ATTN_Q_DIM, tn), lambda i, j: (0, j)),
            pl.BlockSpec((HG_DIM, tn), lambda i, j: (ATTN_Q_DIM // HG_DIM, j)),
        ] + c_in,
        out_specs=[pl.BlockSpec((tm, tn), lambda i, j: (i, j))] + c_out,
        out_shape=[jax.ShapeDtypeStruct((m, d), _BF16)] + c_shapes,
        compiler_params=_compiler_params(2),
        name="gated_merge",
    )(nx.xg, nx.ss, o_attn, o_hgrn, w_in_b, w_in_b, w_branch_b, w_branch_b, *c_args)


def _bucket_table():
    ki = np.arange(2 * ATTN_BLOCK)[:, None]
    qi = np.arange(ATTN_BLOCK)[None, :]
    dist = qi + ATTN_BLOCK - ki
    in_window = (dist >= 0) & (dist < ATTN_BLOCK)
    in_window_first = in_window & (ki >= ATTN_BLOCK)
    dist = np.maximum(dist, 0)
    max_exact = NUM_BUCKETS // 2
    d = np.maximum(dist, 1).astype(np.float64)
    val = np.log(d / max_exact) / math.log(MAX_DISTANCE / max_exact) * (NUM_BUCKETS - max_exact)
    large = max_exact + np.trunc(val).astype(np.int64)
    frac = np.abs(val - np.round(val))[in_window & (dist > max_exact)]
    assert frac.min() > 1e-3
    bucket = np.where(dist < max_exact, dist, np.minimum(large, NUM_BUCKETS - 1))
    return bucket.reshape(-1), in_window_first.reshape(-1), in_window.reshape(-1)


def _bias_kernel(rbt_ref, onehot_ref, mask_ref, o_ref):
    a1, a2, a3 = _split3_bf16(rbt_ref[...])
    lhs = jnp.concatenate([a1, a2, a3], axis=1)
    base = _dot(lhs, onehot_ref[...])
    o_ref[0] = base + mask_ref[0]
    o_ref[1] = base + mask_ref[1]


def _bias_band(rel_bias, *, tn=4096):
    bucket, first, other = _bucket_table()
    n = bucket.shape[0]
    onehot = (np.arange(NUM_BUCKETS)[:, None] == bucket[None, :]).astype(np.float32)
    onehot3 = jnp.asarray(np.concatenate([onehot] * 3, axis=0), dtype=_BF16)
    maskadd = jnp.asarray(np.where(np.stack([first, other])[:, None, :], 0.0, -np.inf), dtype=_F32)
    out = pl.pallas_call(
        _bias_kernel,
        grid=(n // tn,),
        in_specs=[pl.BlockSpec((ATTN_HEADS, NUM_BUCKETS), lambda j: (0, 0)),
                  pl.BlockSpec((3 * NUM_BUCKETS, tn), lambda j: (0, j)),
                  pl.BlockSpec((2, 1, tn), lambda j: (0, 0, j))],
        out_specs=pl.BlockSpec((2, ATTN_HEADS, tn), lambda j: (0, 0, j)),
        out_shape=jax.ShapeDtypeStruct((2, ATTN_HEADS, n), _F32),
        compiler_params=_compiler_params(1),
        name="rel_bias_band",
    )(rel_bias.astype(_F32).T, onehot3, maskadd)
    return out.reshape(2, ATTN_HEADS, 2 * ATTN_BLOCK, ATTN_BLOCK)


def _group_rms(x, blockdiag, gain):
    x2 = x * x
    hi = x2.astype(_BF16)
    lo = (x2 - hi.astype(_F32)).astype(_BF16)
    ss = _dot(hi, blockdiag) + _dot(lo, blockdiag)
    return x * lax.rsqrt(ss * (1.0 / ATTN_HEAD_DIM) + EPS) * gain


def _pair_stack(slab, low_half):
    swapped = pltpu.roll(slab, ATTN_HEAD_DIM, 1)
    zero = jnp.zeros_like(slab)
    even = jnp.concatenate([jnp.where(low_half, slab, zero), jnp.where(low_half, zero, swapped)], axis=0)
    odd = jnp.concatenate([jnp.where(low_half, swapped, zero), jnp.where(low_half, zero, slab)], axis=0)
    return even.astype(_BF16), odd.astype(_BF16)


class _AttnSteps(NamedTuple):
    prologue: Callable[[], None]
    scores: Callable[[int], None]
    finish: Callable[[int], None]
    n_pairs: int


def _attn_steps(sink_ref, q_ref, kvc_ref, kvp_ref, bias_ref, gq_ref, gk_ref, bd_ref, o_ref):
    w = 2 * ATTN_BLOCK
    st = {}

    def prologue():
        bd = bd_ref[...]
        kv = jnp.concatenate([kvp_ref[...], kvc_ref[...]], axis=0)
        kn = _group_rms(kv[:, :ATTN_KV_DIM], bd, gk_ref[...])
        v = kv[:, ATTN_KV_DIM:]
        lane = lax.broadcasted_iota(jnp.int32, (w, V7X_LANES), 1)
        low_half = lane < ATTN_HEAD_DIM
        k_stacks, v_stacks = [], []
        for s in range(ATTN_KV_DIM // V7X_LANES):
            k_stacks.extend(_pair_stack(kn[:, s * V7X_LANES:(s + 1) * V7X_LANES], low_half))
            v_stacks.extend(_pair_stack(v[:, s * V7X_LANES:(s + 1) * V7X_LANES], low_half))
        st["k"], st["v"], st["bd"] = k_stacks, v_stacks, bd

    def kv_head(pair):
        return (pair * V7X_LANES) // (ATTN_GROUP * ATTN_HEAD_DIM)

    def scores(pair):
        s, half = divmod(pair, 2)
        if half == 0:
            q = q_ref[:, s * w:(s + 1) * w]
            st["qn"] = (_group_rms(q, st["bd"], gq_ref[...]) * ATTN_SCALE).astype(_BF16)
        qn = st["qn"][:, half * V7X_LANES:(half + 1) * V7X_LANES]
        st["sc", pair] = _nt_dot(st["k"][kv_head(pair)], qn)

    def finish(pair):
        sc = st.pop(("sc", pair))
        probs = []
        for e in range(2):
            head = 2 * pair + e
            se = sc[e * w:(e + 1) * w] + bias_ref[head]
            sink = sink_ref[head]
            mx = jnp.maximum(jnp.max(se, axis=0, keepdims=True), sink)
            p = jnp.exp(se - mx)
            denom = jnp.sum(p, axis=0, keepdims=True) + jnp.exp(sink - mx)
            probs.append((p * (1.0 / denom)).astype(_BF16))
        o = _tn_dot(jnp.concatenate(probs, axis=0), st["v"][kv_head(pair)])
        o_ref[:, pair * V7X_LANES:(pair + 1) * V7X_LANES] = o.astype(o_ref.dtype)

    return _AttnSteps(prologue, scores, finish, ATTN_HEADS // 2)


def _proj_attn_kernel(*refs, n_cvt):
    a_ref, w_ref, ss_ref, sink_ref, q_ref, kvc_ref, kvp_ref, bias_ref, gq_ref, gk_ref, bd_ref = refs[:11]
    cvt_src = refs[11:11 + n_cvt]
    hg_ref, o_ref = refs[11 + n_cvt:13 + n_cvt]
    cvt_dst = refs[13 + n_cvt:]
    _run_cvts(cvt_src, cvt_dst)
    attn = _attn_steps(sink_ref, q_ref, kvc_ref, kvp_ref, bias_ref, gq_ref, gk_ref, bd_ref, o_ref)
    kc = a_ref.shape[1] // attn.n_pairs
    attn.prologue()
    attn.scores(0)
    for p in range(attn.n_pairs):
        part = _dot(a_ref[:, p * kc:(p + 1) * kc], w_ref[p * kc:(p + 1) * kc, :])
        if p + 1 < attn.n_pairs:
            attn.scores(p + 1)
        if p == 0:
            hg_ref[...] = part
        elif p + 1 < attn.n_pairs:
            hg_ref[...] += part
        else:
            hg_ref[...] = _scale_rows_rms(hg_ref[...] + part, ss_ref, a_ref.shape[1])
        attn.finish(p)


def _proj_hgrn_with_attention(nx, w_in_b, col_block0, qkv, sinks, bias, q_gain, k_gain, batch, seq, *, tm, tn,
                              cvts=()):
    m, k = nx.xg.shape
    tm = min(tm, m)
    nb = seq // ATTN_BLOCK
    grid = (m // tm, HGRN_IN_DIM // tn)
    nj = grid[1]
    assert grid[0] * nj == batch * nb and k % (ATTN_HEADS // 2) == 0
    kv_col = ATTN_Q_DIM // (2 * ATTN_KV_DIM)
    assert ATTN_Q_DIM % (2 * ATTN_KV_DIM) == 0
    blockdiag = jnp.asarray(np.kron(np.eye(2 * ATTN_BLOCK // ATTN_HEAD_DIM), np.ones((ATTN_HEAD_DIM,) * 2)), _BF16)
    tile = lambda g: jnp.tile(g.reshape(1, ATTN_HEAD_DIM).astype(_F32), (1, 2 * ATTN_BLOCK // ATTN_HEAD_DIM))
    blk = lambda i, j: i * nj + j
    first = lambda i, j: (blk(i, j) % nb) == 0
    c_in, c_out, c_shapes, c_args = _cvt_plumbing(cvts, grid[0] * nj, blk)
    return pl.pallas_call(
        functools.partial(_proj_attn_kernel, n_cvt=len(cvts)),
        grid=grid,
        in_specs=[
            pl.BlockSpec((tm, k), lambda i, j: (i, 0)),
            pl.BlockSpec((k, tn), lambda i, j: (0, j + col_block0)),
            pl.BlockSpec((tm, V7X_LANES), lambda i, j: (i, 0)),
            pl.BlockSpec(memory_space=pltpu.SMEM),
            pl.BlockSpec((ATTN_BLOCK, ATTN_Q_DIM), lambda i, j: (blk(i, j), 0)),
            pl.BlockSpec((ATTN_BLOCK, 2 * ATTN_KV_DIM), lambda i, j: (blk(i, j), kv_col)),
            pl.BlockSpec((ATTN_BLOCK, 2 * ATTN_KV_DIM),
                         lambda i, j: (jnp.where(first(i, j), blk(i, j), blk(i, j) - 1), kv_col)),
            pl.BlockSpec((None, ATTN_HEADS, 2 * ATTN_BLOCK, ATTN_BLOCK),
                         lambda i, j: (jnp.where(first(i, j), 0, 1), 0, 0, 0)),
            pl.BlockSpec((1, 2 * ATTN_BLOCK), lambda i, j: (0, 0)),
            pl.BlockSpec((1, 2 * ATTN_BLOCK), lambda i, j: (0, 0)),
            pl.BlockSpec((2 * ATTN_BLOCK, 2 * ATTN_BLOCK), lambda i, j: (0, 0)),
        ] + c_in,
        out_specs=[pl.BlockSpec((tm, tn), lambda i, j: (i, j)),
                   pl.BlockSpec((ATTN_BLOCK, ATTN_Q_DIM), lambda i, j: (blk(i, j), 0))] + c_out,
        out_shape=[jax.ShapeDtypeStruct((m, HGRN_IN_DIM), _F32),
                   jax.ShapeDtypeStruct((m, ATTN_Q_DIM), _BF16)] + c_shapes,
        compiler_params=_compiler_params(2),
        name="proj_hgrn_attn",
    )(nx.xg, w_in_b, nx.ss, sinks.astype(_F32), qkv, qkv, qkv, bias, tile(q_gain), tile(k_gain), blockdiag,
      *c_args)


_HG_LEVELS = tuple(HG_CHUNK >> (i + 1) for i in range(int(math.log2(HG_CHUNK))))
_HG_PIPELINE_LEAD = 4
_HG_MATMUL_LEVELS = tuple(m for m in _HG_LEVELS if m < V7X_F32_SUBLANES)


def _hgrn_tables():
    c = HG_CHUNK
    t = np.arange(c)
    blocks = [(t[None, :] <= t[:, None])]
    masks = [np.eye(c, dtype=bool)]
    for m in _HG_LEVELS:
        ref = (t // (2 * m)) * (2 * m) + m - 1
        upper = (t % (2 * m)) >= m
        if m in _HG_MATMUL_LEVELS:
            up_rows = upper[:, None] & (t[None, :] > ref[:, None]) & (t[None, :] <= t[:, None])
            lo_rows = (~upper)[:, None] & (t[None, :] > t[:, None]) & (t[None, :] <= ref[:, None])
            blocks.append(up_rows | lo_rows)
        same = (t[:, None] // (2 * m)) == (t[None, :] // (2 * m))
        masks.append(same & upper[:, None] & (~upper)[None, :])
    sums = np.concatenate(blocks, axis=0).astype(np.float32)
    return np.concatenate([sums] * 3, axis=1), np.stack(masks).astype(np.float32)


def _hgrn_kernel(p_ref, lbl_ref, gain_ref, sums_ref, masks_ref, o_ref, state_ref, *, layer, rows):
    c = HG_CHUNK

    @pl.when(pl.program_id(1) == 0)
    def _():
        state_ref[...] = jnp.zeros_like(state_ref)

    lg = lbl_ref[...]
    e = jnp.exp(lg - jnp.max(lg, axis=0, keepdims=True))
    sm = e / jnp.sum(e, axis=0, keepdims=True)
    lb_all = jnp.zeros_like(sm[0:1])
    for i in range(1, layer + 1):
        lb_all = lb_all + sm[i:i + 1]

    row = lax.broadcasted_iota(jnp.int32, (c, 1), 0)
    gain = gain_ref[...]
    sums = sums_ref[...]

    def chunk_body(ci, carry):
        r0 = pl.multiple_of(ci * c, c)
        st = [dict() for _ in range(HG_HEADS)]

        def seg(h, which):
            return p_ref[pl.ds(r0, c), which * HG_DIM + h * HG_DK:which * HG_DIM + (h + 1) * HG_DK]

        def gate_math(h):
            lb = lb_all[:, h * HG_DK:(h + 1) * HG_DK]
            hq = seg(h, 0)
            forget = lb + (1.0 - lb) * jax.nn.sigmoid(seg(h, 1))
            st[h]["kk"] = 1.0 - forget
            st[h]["qf"] = hq * jax.nn.sigmoid(hq)
            return jnp.concatenate(_split3_bf16(jnp.log(forget)), axis=0)

        def gates(h):
            if h % 2 == 0:
                both = _dot(sums, jnp.concatenate([gate_math(h), gate_math(h + 1)], axis=1))
                st[h]["expo"], st[h + 1]["expo"] = both[:, :HG_DK], both[:, HG_DK:]

        def level_decay(expo, m):
            if m in _HG_MATMUL_LEVELS:
                k = 1 + _HG_MATMUL_LEVELS.index(m)
                return jnp.exp(expo[k * c:(k + 1) * c])
            bcum = expo[0:c]
            blocks = []
            for s0 in range(0, c, 2 * m):
                blocks.append(-jnp.abs(bcum[s0:s0 + 2 * m] - bcum[s0 + m - 1:s0 + m]))
            return jnp.exp(jnp.concatenate(blocks, axis=0))

        def products(h):
            d = st[h]
            qf, kk = d.pop("qf"), d.pop("kk")
            d["v"] = seg(h, 2).astype(_BF16)
            expo = d.pop("expo")
            bcum = expo[0:c]
            e_cum = jnp.exp(bcum)
            e_end = jnp.exp(bcum[c - 1:c] - bcum)
            state_t = state_ref[h]
            d["o_inter"] = _nt_dot((qf * e_cum).astype(_BF16), state_t.astype(_BF16))
            parts = [_nt_dot(qf.astype(_BF16), kk.astype(_BF16))]
            for m in _HG_LEVELS:
                upper = (row & m) != 0
                z = (jnp.where(upper, qf, kk) * level_decay(expo, m)).astype(_BF16)
                parts.append(_nt_dot(z, z))
            d["parts"] = parts
            state_ref[h] = state_t * e_cum[c - 1:c] + _tn_dot(d["v"], (kk * e_end).astype(_BF16))

        def output(h):
            d = st[h]
            parts = d.pop("parts")
            scores = parts[0] * masks_ref[0]
            for li in range(len(_HG_LEVELS)):
                scores = scores + parts[1 + li] * masks_ref[1 + li]
            o = d.pop("o_inter") + _dot(scores.astype(_BF16), d.pop("v"))
            hg = seg(h, 3)
            y = o * lax.rsqrt(jnp.mean(o * o, axis=-1, keepdims=True) + EPS) * gain
            y = y * (hg * jax.nn.sigmoid(hg))
            o_ref[pl.ds(r0, c), h * HG_DV:(h + 1) * HG_DV] = y.astype(o_ref.dtype)

        lead = _HG_PIPELINE_LEAD
        for h in range(-2 * lead, HG_HEADS):
            if 0 <= h + 2 * lead < HG_HEADS:
                gates(h + 2 * lead)
            if 0 <= h + lead < HG_HEADS:
                products(h + lead)
            if 0 <= h:
                output(h)
        return carry

    lax.fori_loop(0, rows // c, chunk_body, 0)


def _hgrn(p, lb_logits, norm_gain, layer, batch, seq, *, rows=512):
    rows = min(rows, seq)
    steps = seq // rows
    depth = lb_logits.shape[0]
    sums, masks = _hgrn_tables()
    return pl.pallas_call(
        functools.partial(_hgrn_kernel, layer=layer, rows=rows),
        grid=(batch, steps),
        in_specs=[
            pl.BlockSpec((rows, HGRN_IN_DIM), lambda b, s: (b * steps + s, 0)),
            pl.BlockSpec((depth, HG_DIM), lambda b, s: (0, 0)),
            pl.BlockSpec((1, HG_DV), lambda b, s: (0, 0)),
            pl.BlockSpec(sums.shape, lambda b, s: (0, 0)),
            pl.BlockSpec(masks.shape, lambda b, s: (0, 0, 0)),
        ],
        out_specs=pl.BlockSpec((rows, HG_DIM), lambda b, s: (b * steps + s, 0)),
        out_shape=jax.ShapeDtypeStruct((batch * seq, HG_DIM), _BF16),
        scratch_shapes=[pltpu.VMEM((HG_HEADS, HG_DV, HG_DK), _F32)],
        compiler_params=_compiler_params(2),
        name="hgrn2_scan",
    )(p, lb_logits.astype(_F32), norm_gain.reshape(1, HG_DV).astype(_F32),
      jnp.asarray(sums, _BF16), jnp.asarray(masks, _F32))


def kernel(x, attn_norm_gain, w_in, q_norm_gain, k_norm_gain, attn_sinks, rel_bias,
           hgrn_lb_logits, hgrn_norm_gain, w_branch, w_out, mlp_norm_gain, w_up, w_down):
    batch, seq, d = x.shape
    depth = w_in.shape[0]
    m = batch * seq
    xf = x.reshape(m, d).astype(_F32)
    bias = _bias_band(rel_bias)
    t = _TILES
    hg_col0 = QKV_DIM // t.proj_tn
    gate_col0 = QKV_DIM + HGRN_IN_DIM
    assert QKV_DIM % t.proj_tn == 0

    w_in_b = w_in[0, :, :gate_col0].astype(_BF16)
    nx = _normed_input(xf, attn_norm_gain[0])
    for l in range(depth):
        qkv, = _matmul(nx.xg, w_in_b, n_out=QKV_DIM, tm=t.tm, tn=t.qkv_tn, out_dtype=_F32, row_ss=nx.ss,
                       name="proj_qkv")
        first = () if l else (_Cvt(w_branch, 0), _Cvt(w_out, 0),
                              _Cvt(w_in, 0, col0=gate_col0, ncols=2 * d, bcols=t.proj_tn))
        hgp, o_attn, *first_b = _proj_hgrn_with_attention(
            nx, w_in_b, hg_col0, qkv, attn_sinks[l], bias, q_norm_gain[l], k_norm_gain[l], batch, seq,
            tm=t.tm, tn=t.proj_tn, cvts=first)
        if first_b:
            w_br_b, w_o_b, w_gate0_b = first_b
        o_hgrn = _hgrn(hgp, hgrn_lb_logits, hgrn_norm_gain[l], l, batch, seq)
        w_gate_b, g0 = (w_in_b, gate_col0) if l else (w_gate0_b, 0)
        merged, w_u_b = _merge(nx, o_attn, o_hgrn, w_gate_b, w_br_b, gate_col0=g0,
                               tm=t.tm, tn=t.merge_tn, cvts=(_Cvt(w_up, l),))
        xf, xg, ss = _matmul(merged, w_o_b, n_out=d, tm=t.tm, tn=t.proj_tn, out_dtype=_F32, residual=xf,
                             norm_gain=mlp_norm_gain[l], name="out_proj")

        nxt = () if l + 1 == depth else (_Cvt(w_in, l + 1), _Cvt(w_branch, l + 1), _Cvt(w_out, l + 1))
        u, w_d_b, *nxt_b = _matmul(xg, w_u_b, n_out=w_up.shape[2], tm=t.tm, tn=t.up_tn, out_dtype=_BF16,
                                   row_ss=ss, epilogue=_relu2, cvts=(_Cvt(w_down, l),) + nxt, name="mlp_up")
        if l + 1 == depth:
            xf, = _matmul_ktiled_res(u, w_d_b, xf, tm=t.tm, tn=t.down_tn, tk=t.down_tk, name="mlp_down")
        else:
            w_in_b, w_br_b, w_o_b = nxt_b
            xf, xg, ss = _matmul_ktiled_res(u, w_d_b, xf, tm=t.tm, tn=t.down_tn, tk=t.down_tk,
                                            norm_gain=attn_norm_gain[l + 1], name="mlp_down")
            nx = _Normed(xg, ss)
    return xf.reshape(batch, seq, d).astype(x.dtype)
```
